```python
import jax, jax.numpy as jnp
from jax import lax
import numpy as np

D_MODEL = 2048
BATCH = 4
SEQ = 2048
DEPTH = 4

POOL_WINDOWS = (2, 4, 8, 16)
POOL_GROUPS = len(POOL_WINDOWS)
POOL_GROUP_DIM = D_MODEL // 8
POOL_WIDTH = POOL_GROUPS * POOL_GROUP_DIM
FOX_HEAD_DIM = 128
FOX_HEADS = D_MODEL // (2 * FOX_HEAD_DIM)
FOX_WIDTH = FOX_HEADS * FOX_HEAD_DIM
RET_HEAD_DIM = 128
RET_HEADS = D_MODEL // (2 * RET_HEAD_DIM)
RET_WIDTH = RET_HEADS * RET_HEAD_DIM
ROPE_BASE = 10000.0
N_BRANCHES = 3
D_FF = ((8 * D_MODEL // 3 + 255) // 256) * 256
BLOCK = 128
RMS_EPS = 1e-6
GN_EPS = 1e-5
FORGET_BIAS_MEAN = 2.0

IN_SPLITS = (POOL_WIDTH,
             FOX_WIDTH, FOX_WIDTH, FOX_WIDTH,
             FOX_HEADS,
             RET_WIDTH, RET_WIDTH, RET_WIDTH,
             RET_WIDTH,
             N_BRANCHES * D_MODEL)
IN_WIDTH = sum(IN_SPLITS)
IN_INDICES = tuple(int(i) for i in np.cumsum(IN_SPLITS)[:-1])

kernel_name = "hybrid_pool_fox_retention_macaron"

F32 = jnp.float32


def rms_norm(x, g):
    xf = x.astype(F32)
    y = xf * lax.rsqrt(jnp.mean(xf * xf, axis=-1, keepdims=True) + RMS_EPS) * g.astype(F32)
    return y.astype(x.dtype)


def swiglu(h, w13, w2):
    a, b = jnp.split(h @ w13, 2, axis=-1)
    return (jax.nn.silu(a) * b) @ w2


def rotary(x, pos):
    half = x.shape[-1] // 2
    inv_freq = ROPE_BASE ** (-jnp.arange(half, dtype=F32) / half)
    ang = pos.astype(F32)[:, None] * inv_freq[None, :]
    cos = jnp.cos(ang)[None, :, None, :]
    sin = jnp.sin(ang)[None, :, None, :]
    xf = x.astype(F32)
    x1, x2 = xf[..., :half], xf[..., half:]
    return jnp.concatenate([x1 * cos - x2 * sin, x1 * sin + x2 * cos], axis=-1).astype(x.dtype)


def pool_mixer(u, w_pool, scale):
    b, s, _ = u.shape
    ug = u.reshape(b, s, POOL_GROUPS, POOL_GROUP_DIM).astype(F32)
    csum = jnp.cumsum(ug, axis=1)
    t = jnp.arange(s)
    means = []
    for g, w in enumerate(POOL_WINDOWS):
        cg = jnp.pad(csum[:, :, g], ((0, 0), (w, 0), (0, 0)))
        win_sum = cg[:, w:] - cg[:, :s]
        cnt = jnp.minimum(t + 1, w).astype(F32)[None, :, None]
        means.append(win_sum / cnt)
    pooled = jnp.stack(means, axis=2) - ug
    mixed = jnp.einsum('bsgi,gio->bsgo', pooled, w_pool.astype(F32))
    out = mixed.reshape(b, s, POOL_WIDTH) * scale.astype(F32)
    return out.astype(u.dtype)


def forgetting_attention(q, k, v, f_logit, f_bias):
    b, s, h, dh = q.shape
    log_f = jax.nn.log_sigmoid(f_logit.astype(F32) + f_bias.astype(F32))
    cum_f = jnp.cumsum(log_f, axis=1).transpose(0, 2, 1)
    qh, kh, vh = (a.transpose(0, 2, 1, 3) for a in (q, k, v))
    scale = dh ** -0.5
    local = jnp.arange(BLOCK)
    causal = local[:, None] >= local[None, :]
    outs = []
    for i in range(s // BLOCK):
        q0, q1 = i * BLOCK, (i + 1) * BLOCK
        kb, vb = kh[:, :, :q1], vh[:, :, :q1]
        logits = jnp.einsum('bhqd,bhkd->bhqk', qh[:, :, q0:q1], kb,
                            preferred_element_type=F32) * scale
        logits = logits + cum_f[:, :, q0:q1, None] - cum_f[:, :, None, :q1]
        mask = jnp.concatenate([jnp.ones((BLOCK, q0), dtype=bool), causal], axis=1)
        logits = jnp.where(mask[None, None], logits, -jnp.inf)
        p = jax.nn.softmax(logits, axis=-1)
        outs.append(jnp.einsum('bhqk,bhkd->bqhd', p.astype(vb.dtype), vb))
    return jnp.concatenate(outs, axis=1).reshape(b, s, h * dh)


def retention(q, k, v, gate):
    b, s, h, dh = q.shape
    n = s // BLOCK
    log_gamma = jnp.log1p(-jnp.power(2.0, -5.0 - jnp.arange(h, dtype=F32)))
    pos = jnp.arange(BLOCK, dtype=F32)
    diff = pos[:, None] - pos[None, :]
    inner_decay = jnp.where(diff[None] >= 0,
                            jnp.exp(jnp.maximum(diff, 0.0)[None] * log_gamma[:, None, None]),
                            0.0)
    xi = jnp.exp((pos + 1.0)[None, :] * log_gamma[:, None])
    zeta = jnp.exp((BLOCK - 1.0 - pos)[None, :] * log_gamma[:, None])
    chunk_decay = jnp.exp(BLOCK * log_gamma)

    qc = q.astype(F32).reshape(b, n, BLOCK, h, dh)
    kc = k.astype(F32).reshape(b, n, BLOCK, h, dh) * (dh ** -0.5)
    vc = v.astype(F32).reshape(b, n, BLOCK, h, dh)

    scores = jnp.einsum('bnqhd,bnkhd->bnhqk', qc, kc) * inner_decay[None, None]
    o_inner = jnp.einsum('bnhqk,bnkhe->bnqhe', scores, vc)

    kv = jnp.einsum('bnkhd,hk,bnkhe->bnhde', kc, zeta, vc)

    def step(state, kv_chunk):
        return chunk_decay[None, :, None, None] * state + kv_chunk, state

    _, prev = lax.scan(step, jnp.zeros((b, h, dh, dh), F32), kv.transpose(1, 0, 2, 3, 4))
    prev = prev.transpose(1, 0, 2, 3, 4)
    o_cross = jnp.einsum('bnqhd,hq,bnhde->bnqhe', qc, xi, prev)

    o = (o_inner + o_cross).reshape(b, s, h, dh)
    mu = jnp.mean(o, axis=-1, keepdims=True)
    var = jnp.mean(jnp.square(o - mu), axis=-1, keepdims=True)
    o = ((o - mu) * lax.rsqrt(var + GN_EPS)).reshape(b, s, h * dh)
    return (jax.nn.silu(gate.astype(F32)) * o).astype(q.dtype)


def hybrid_mixer(hn, w_in, forget_bias, pool_w, pool_scale,
                 w_branch_pool, w_branch_fox, w_branch_ret, w_out):
    b, s, d = hn.shape
    z = hn @ w_in
    (u_pool, fq, fk, fv, f_logit, rq, rk, rv, rgate, gate_logits) = jnp.split(z, IN_INDICES, axis=-1)
    pos = jnp.arange(s)

    y_pool = pool_mixer(u_pool, pool_w, pool_scale)

    hsplit = lambda a, nh, hd: a.reshape(b, s, nh, hd)
    y_fox = forgetting_attention(hsplit(fq, FOX_HEADS, FOX_HEAD_DIM),
                                 hsplit(fk, FOX_HEADS, FOX_HEAD_DIM),
                                 hsplit(fv, FOX_HEADS, FOX_HEAD_DIM),
                                 f_logit, forget_bias)

    y_ret = retention(rotary(hsplit(rq, RET_HEADS, RET_HEAD_DIM), pos),
                      rotary(hsplit(rk, RET_HEADS, RET_HEAD_DIM), pos),
                      hsplit(rv, RET_HEADS, RET_HEAD_DIM), rgate)

    gates = jax.nn.sigmoid(gate_logits.astype(F32)).reshape(b, s, N_BRANCHES, d)
    merged = (gates[:, :, 0] * (y_pool @ w_branch_pool).astype(F32)
              + gates[:, :, 1] * (y_fox @ w_branch_fox).astype(F32)
              + gates[:, :, 2] * (y_ret @ w_branch_ret).astype(F32))
    return merged.astype(hn.dtype) @ w_out


def setup_inputs(seed: int = 0) -> dict:
    key = jax.random.key(seed)
    ks = jax.random.split(key, 20)

    def nrm(k, shape, fan_in):
        return jax.random.normal(k, shape, F32) * (fan_in ** -0.5)

    def gain(k, shape):
        return 1.0 + 0.05 * jax.random.normal(k, shape, F32)

    return {
        "x": jax.random.normal(ks[0], (BATCH, SEQ, D_MODEL), F32),
        "ffn1_norm": gain(ks[1], (DEPTH, D_MODEL)),
        "ffn1_w13": nrm(ks[2], (DEPTH, D_MODEL, 2 * D_FF), D_MODEL),
        "ffn1_w2": nrm(ks[3], (DEPTH, D_FF, D_MODEL), D_FF),
        "mix_norm": gain(ks[4], (DEPTH, D_MODEL)),
        "w_in": nrm(ks[5], (DEPTH, D_MODEL, IN_WIDTH), D_MODEL),
        "forget_bias": FORGET_BIAS_MEAN + 0.5 * jax.random.normal(ks[6], (DEPTH, FOX_HEADS), F32),
        "pool_w": nrm(ks[7], (DEPTH, POOL_GROUPS, POOL_GROUP_DIM, POOL_GROUP_DIM), POOL_GROUP_DIM),
        "pool_scale": 1.0 + 0.1 * jax.random.normal(ks[8], (DEPTH, POOL_WIDTH), F32),
        "w_branch_pool": nrm(ks[9], (DEPTH, POOL_WIDTH, D_MODEL), POOL_WIDTH),
        "w_branch_fox": nrm(ks[10], (DEPTH, FOX_WIDTH, D_MODEL), FOX_WIDTH),
        "w_branch_ret": nrm(ks[11], (DEPTH, RET_WIDTH, D_MODEL), RET_WIDTH),
        "w_out": nrm(ks[12], (DEPTH, D_MODEL, D_MODEL), D_MODEL),
        "ffn2_norm": gain(ks[13], (DEPTH, D_MODEL)),
        "ffn2_w13": nrm(ks[14], (DEPTH, D_MODEL, 2 * D_FF), D_MODEL),
        "ffn2_w2": nrm(ks[15], (DEPTH, D_FF, D_MODEL), D_FF),
        "final_norm": gain(ks[16], (D_MODEL,)),
    }


def reference(x, ffn1_norm, ffn1_w13, ffn1_w2, mix_norm, w_in, forget_bias, pool_w, pool_scale,
              w_branch_pool, w_branch_fox, w_branch_ret, w_out, ffn2_norm, ffn2_w13, ffn2_w2,
              final_norm):
    for l in range(DEPTH):
        x = x + 0.5 * swiglu(rms_norm(x, ffn1_norm[l]), ffn1_w13[l], ffn1_w2[l])
        x = x + hybrid_mixer(rms_norm(x, mix_norm[l]), w_in[l], forget_bias[l], pool_w[l],
                             pool_scale[l], w_branch_pool[l], w_branch_fox[l], w_branch_ret[l],
                             w_out[l])
        x = x + 0.5 * swiglu(rms_norm(x, ffn2_norm[l]), ffn2_w13[l], ffn2_w2[l])
    return rms_norm(x, final_norm)
```

```python
import functools

import jax
import jax.numpy as jnp
from jax import lax
from jax.experimental import pallas as pl
from jax.experimental.pallas import tpu as pltpu

F32 = jnp.float32
BF16 = jnp.bfloat16

RMS_EPS = 1e-6
GN_EPS = 1e-5
ROPE_BASE = 10000.0
POOL_WINDOWS = (2, 4, 8, 16)
POOL_HALO = 16
HEAD_DIM = 128
CHUNK = 128
N_BRANCHES = 3

LANES = 128
SUBLANES = 8
VMEM_LIMIT_BYTES = 56 * 1024 * 1024


def _params(*semantics):
    return pltpu.CompilerParams(dimension_semantics=semantics,
                                vmem_limit_bytes=VMEM_LIMIT_BYTES)


def _rms(x, g):
    return x * lax.rsqrt(jnp.mean(x * x, axis=-1, keepdims=True) + RMS_EPS) * g


def _silu(x):
    return x * jax.nn.sigmoid(x)


def _dot(a, b):
    return jnp.dot(a, b, preferred_element_type=F32)


def _dot_nt(a, b):
    return lax.dot_general(a, b, (((1,), (1,)), ((), ())), preferred_element_type=F32)


def _ffn_kernel(*refs, final_norm):
    if final_norm:
        x_ref, g_ref, w1_ref, w3_ref, w2_ref, gf_ref, o_ref, xn_ref = refs
    else:
        x_ref, g_ref, w1_ref, w3_ref, w2_ref, o_ref, xn_ref = refs
    f = pl.program_id(1)

    @pl.when(f == 0)
    def _():
        x = x_ref[...]
        xn_ref[...] = _rms(x, g_ref[...]).astype(BF16)
        o_ref[...] = x

    xn = xn_ref[...]
    h = (_silu(_dot(xn, w1_ref[...])) * _dot(xn, w3_ref[...])).astype(BF16)
    o_ref[...] += 0.5 * _dot(h, w2_ref[...])

    if final_norm:
        @pl.when(f == pl.num_programs(1) - 1)
        def _():
            o_ref[...] = _rms(o_ref[...], gf_ref[...])


def _ffn(x, norm, w13, w2, layer, final_gain=None, *, tm=512, tf=512):
    t, d = x.shape
    d_ff = w2.shape[1]
    nf = d_ff // tf
    in_specs = [
        pl.BlockSpec((tm, d), lambda m, f: (m, 0)),
        pl.BlockSpec((None, 1, d), lambda m, f: (layer, 0, 0)),
        pl.BlockSpec((None, d, tf), lambda m, f: (layer, 0, f)),
        pl.BlockSpec((None, d, tf), lambda m, f: (layer, 0, f + nf)),
        pl.BlockSpec((None, tf, d), lambda m, f: (layer, f, 0)),
    ]
    args = [x, norm, w13, w13, w2]
    if final_gain is not None:
        in_specs.append(pl.BlockSpec((1, d), lambda m, f: (0, 0)))
        args.append(final_gain)
    return pl.pallas_call(
        functools.partial(_ffn_kernel, final_norm=final_gain is not None),
        grid=(t // tm, nf),
        in_specs=in_specs,
        out_specs=pl.BlockSpec((tm, d), lambda m, f: (m, 0)),
        out_shape=jax.ShapeDtypeStruct((t, d), F32),
        scratch_shapes=[pltpu.VMEM((tm, d), BF16)],
        compiler_params=_params("parallel", "arbitrary"),
        name="ffn",
    )(*args)


def _norm_matmul_kernel(x_ref, g_ref, w_ref, o_ref, xn_ref):
    @pl.when(pl.program_id(1) == 0)
    def _():
        xn_ref[...] = _rms(x_ref[...], g_ref[...]).astype(BF16)

    o_ref[...] = _dot(xn_ref[...], w_ref[...]).astype(o_ref.dtype)


def _norm_matmul(x, norm, w, layer, out_dtype, *, tm=1024, tn=1024):
    t, d = x.shape
    n = w.shape[-1]
    tn = min(tn, n)
    return pl.pallas_call(
        _norm_matmul_kernel,
        grid=(t // tm, n // tn),
        in_specs=[
            pl.BlockSpec((tm, d), lambda m, j: (m, 0)),
            pl.BlockSpec((None, 1, d), lambda m, j: (layer, 0, 0)),
            pl.BlockSpec((None, d, tn), lambda m, j: (layer, 0, j)),
        ],
        out_specs=pl.BlockSpec((tm, tn), lambda m, j: (m, j)),
        out_shape=jax.ShapeDtypeStruct((t, n), out_dtype),
        scratch_shapes=[pltpu.VMEM((tm, d), BF16)],
        compiler_params=_params("parallel", "arbitrary"),
        name="in_proj",
    )(x, norm, w)


def _forget_cumsum_kernel(fl_ref, b_ref, cf_ref, cft_ref, *, heads, tk):
    z = fl_ref[...] + b_ref[...]
    x = jnp.minimum(z, 0.0) - jnp.log1p(jnp.exp(-jnp.abs(z)))
    s = x.shape[0]
    row = lax.broadcasted_iota(jnp.int32, x.shape, 0)
    shift = 1
    while shift < s:
        if shift < SUBLANES:
            prev = jnp.where(row >= shift, pltpu.roll(x, shift, axis=0), 0.0)
        else:
            prev = jnp.concatenate([jnp.zeros((shift, x.shape[1]), F32), x[:s - shift]], axis=0)
        x = x + prev
        shift *= 2
    cf_ref[...] = x
    xt = x.T
    for j in range(s // tk):
        cft_ref[j] = xt[:heads, j * tk:(j + 1) * tk]


def _forget_cumsum(fl, bias, layer, *, heads, tk):
    b, s, w = fl.shape
    return pl.pallas_call(
        functools.partial(_forget_cumsum_kernel, heads=heads, tk=tk),
        grid=(b,),
        in_specs=[
            pl.BlockSpec((None, s, w), lambda i: (i, 0, 0)),
            pl.BlockSpec((None, 1, w), lambda i: (layer, 0, 0)),
        ],
        out_specs=[
            pl.BlockSpec((None, s, w), lambda i: (i, 0, 0)),
            pl.BlockSpec((None, s // tk, heads, tk), lambda i: (i, 0, 0, 0)),
        ],
        out_shape=[
            jax.ShapeDtypeStruct((b, s, w), F32),
            jax.ShapeDtypeStruct((b, s // tk, heads, tk), F32),
        ],
        compiler_params=_params("parallel"),
        name="forget_cumsum",
    )(fl, bias)


def _fox_kernel(q_ref, k_ref, v_ref, cf_ref, cft_ref, o_ref, *, tq):
    i = pl.program_id(1)
    heads = q_ref.shape[-1] // HEAD_DIM
    scale = HEAD_DIM ** -0.5
    causal = (lax.broadcasted_iota(jnp.int32, (tq, tq), 0)
              >= lax.broadcasted_iota(jnp.int32, (tq, tq), 1))

    for h in range(heads):
        lanes = slice(h * HEAD_DIM, (h + 1) * HEAD_DIM)
        q = q_ref[:, lanes]
        fq = cf_ref[:, h:h + 1]

        def block(j, carry, masked):
            m, l, acc = carry
            k0 = pl.multiple_of(j * tq, tq)
            kb = k_ref[pl.ds(k0, tq), lanes]
            vb = v_ref[pl.ds(k0, tq), lanes]
            fk = cft_ref[j, h:h + 1, :]
            s = _dot_nt(q, kb) * scale + fq - fk
            if masked:
                s = jnp.where(causal, s, -jnp.inf)
            m_new = jnp.maximum(m, jnp.max(s, axis=-1, keepdims=True))
            alpha = jnp.exp(m - m_new)
            p = jnp.exp(s - m_new)
            l = alpha * l + jnp.sum(p, axis=-1, keepdims=True)
            acc = alpha * acc + _dot(p.astype(BF16), vb)
            return m_new, l, acc

        init = (jnp.full((tq, 1), -jnp.inf, F32), jnp.zeros((tq, 1), F32),
                jnp.zeros((tq, HEAD_DIM), F32))
        carry = lax.fori_loop(0, i, functools.partial(block, masked=False), init)
        _, l, acc = block(i, carry, masked=True)
        o_ref[:, lanes] = (acc / l).astype(o_ref.dtype)


def _fox(zb, cf, cft, *, width, tq):
    b, s, _ = zb.shape
    heads = width // HEAD_DIM
    return pl.pallas_call(
        functools.partial(_fox_kernel, tq=tq),
        grid=(b, s // tq),
        in_specs=[
            pl.BlockSpec((None, tq, width), lambda bi, i: (bi, i, 0)),
            pl.BlockSpec((None, s, width), lambda bi, i: (bi, 0, 1)),
            pl.BlockSpec((None, s, width), lambda bi, i: (bi, 0, 2)),
            pl.BlockSpec((None, tq, LANES), lambda bi, i: (bi, i, 0)),
            pl.BlockSpec((None, s // tq, heads, tq), lambda bi, i: (bi, 0, 0, 0)),
        ],
        out_specs=pl.BlockSpec((None, tq, width), lambda bi, i: (bi, i, 0)),
        out_shape=jax.ShapeDtypeStruct((b, s, width), BF16),
        compiler_params=_params("parallel", "arbitrary"),
        name="fox",
    )(zb, zb, zb, cf, cft)


def _ret_kernel(q_ref, k_ref, v_ref, gate_ref, cos_ref, sin_ref, dm_ref, xi_ref, zeta_ref,
                cd_ref, o_ref, state_ref, *, chunks):
    @pl.when(pl.program_id(2) == 0)
    def _():
        state_ref[...] = jnp.zeros_like(state_ref)

    for c in range(chunks):
        r = slice(c * CHUNK, (c + 1) * CHUNK)
        cos, sin = cos_ref[r, :], sin_ref[r, :]
        q, k = q_ref[r, :], k_ref[r, :]
        q = q * cos + pltpu.roll(q, HEAD_DIM // 2, axis=1) * sin
        k = (k * cos + pltpu.roll(k, HEAD_DIM // 2, axis=1) * sin) * (HEAD_DIM ** -0.5)
        v = v_ref[r, :]
        state = state_ref[...]
        scores = _dot_nt(q.astype(BF16), k.astype(BF16)) * dm_ref[...]
        o = _dot(scores.astype(BF16), v)
        o = o + _dot((q * xi_ref[...]).astype(BF16), state.astype(BF16))
        kz_t = (k * zeta_ref[...]).T.astype(BF16)
        state_ref[...] = cd_ref[0:1, :] * state + _dot(kz_t, v)
        mu = jnp.mean(o, axis=-1, keepdims=True)
        var = jnp.mean(jnp.square(o - mu), axis=-1, keepdims=True)
        o = (o - mu) * lax.rsqrt(var + GN_EPS)
        o_ref[r, :] = (_silu(gate_ref[r, :]) * o).astype(o_ref.dtype)


def _retention(za, zb, tables, *, q_blk, k_blk, gate_blk, v_blk, heads, tr):
    b, s, _ = za.shape
    cos, sin, dmat, xi, zeta, cdec = tables
    tok = lambda off: pl.BlockSpec((None, tr, HEAD_DIM), lambda bi, h, i: (bi, i, off + h))
    per_head = lambda rows: pl.BlockSpec((None, rows, HEAD_DIM), lambda bi, h, i: (h, 0, 0))
    pos = pl.BlockSpec((tr, HEAD_DIM), lambda bi, h, i: (i, 0))
    return pl.pallas_call(
        functools.partial(_ret_kernel, chunks=tr // CHUNK),
        grid=(b, heads, s // tr),
        in_specs=[tok(q_blk), tok(k_blk), tok(v_blk), tok(gate_blk), pos, pos,
                  per_head(CHUNK), per_head(CHUNK), per_head(CHUNK), per_head(SUBLANES)],
        out_specs=pl.BlockSpec((None, tr, HEAD_DIM), lambda bi, h, i: (bi, i, h)),
        out_shape=jax.ShapeDtypeStruct((b, s, heads * HEAD_DIM), BF16),
        scratch_shapes=[pltpu.VMEM((HEAD_DIM, HEAD_DIM), F32)],
        compiler_params=_params("parallel", "parallel", "arbitrary"),
        name="retention",
    )(za, za, zb, za, cos, sin, dmat, xi, zeta, cdec)


def _retention_tables(s, heads):
    half = HEAD_DIM // 2
    inv_freq = ROPE_BASE ** (-jnp.arange(half, dtype=F32) / half)
    ang = jnp.arange(s, dtype=F32)[:, None] * inv_freq[None, :]
    cos = jnp.concatenate([jnp.cos(ang), jnp.cos(ang)], axis=-1)
    sin = jnp.concatenate([-jnp.sin(ang), jnp.sin(ang)], axis=-1)
    log_gamma = jnp.log1p(-jnp.power(2.0, -5.0 - jnp.arange(heads, dtype=F32)))
    pos = jnp.arange(CHUNK, dtype=F32)
    diff = pos[:, None] - pos[None, :]
    dmat = jnp.where(diff[None] >= 0,
                     jnp.exp(jnp.maximum(diff, 0.0)[None] * log_gamma[:, None, None]), 0.0)
    xi = jnp.exp((pos + 1.0)[None, :] * log_gamma[:, None])
    zeta = jnp.exp((CHUNK - 1.0 - pos)[None, :] * log_gamma[:, None])
    cdec = jnp.exp(CHUNK * log_gamma)
    wide = lambda a: jnp.broadcast_to(a[:, :, None], (heads, CHUNK, HEAD_DIM))
    return (cos, sin, dmat, wide(xi), wide(zeta),
            jnp.broadcast_to(cdec[:, None, None], (heads, SUBLANES, HEAD_DIM)))


def _pool_kernel(u_ref, w_ref, sc_ref, o_ref, ext_ref, *, ts):
    s = pl.program_id(1)
    gd = u_ref.shape[-1] // len(POOL_WINDOWS)

    @pl.when(s == 0)
    def _():
        ext_ref[0:POOL_HALO, :] = jnp.zeros((POOL_HALO, ext_ref.shape[1]), F32)

    @pl.when(s > 0)
    def _():
        ext_ref[0:POOL_HALO, :] = ext_ref[ts:ts + POOL_HALO, :]

    ext_ref[POOL_HALO:POOL_HALO + ts, :] = u_ref[...]
    t = s * ts + lax.broadcasted_iota(jnp.int32, (ts, 1), 0)
    for g, w in enumerate(POOL_WINDOWS):
        cols = slice(g * gd, (g + 1) * gd)
        u = ext_ref[POOL_HALO:POOL_HALO + ts, cols]
        win = u
        for k in range(1, w):
            win = win + ext_ref[POOL_HALO - k:POOL_HALO - k + ts, cols]
        cnt = jnp.minimum(t + 1, w).astype(F32)
        pooled = win / cnt - u
        mixed = _dot(pooled.astype(BF16), w_ref[g])
        o_ref[:, cols] = (mixed * sc_ref[:, cols]).astype(o_ref.dtype)


def _pool(za, pool_w, pool_scale, layer, *, width, ts):
    b, s, _ = za.shape
    groups, gd = pool_w.shape[1], pool_w.shape[2]
    return pl.pallas_call(
        functools.partial(_pool_kernel, ts=ts),
        grid=(b, s // ts),
        in_specs=[
            pl.BlockSpec((None, ts, width), lambda bi, i: (bi, i, 0)),
            pl.BlockSpec((None, groups, gd, gd), lambda bi, i: (layer, 0, 0, 0)),
            pl.BlockSpec((None, 1, width), lambda bi, i: (layer, 0, 0)),
        ],
        out_specs=pl.BlockSpec((None, ts, width), lambda bi, i: (bi, i, 0)),
        out_shape=jax.ShapeDtypeStruct((b, s, width), BF16),
        scratch_shapes=[pltpu.VMEM((POOL_HALO + ts, width), F32)],
        compiler_params=_params("parallel", "arbitrary"),
        name="pool",
    )(za, pool_w, pool_scale)


def _merge_kernel(yp_ref, yf_ref, yr_ref, g0_ref, g1_ref, g2_ref, x_ref,
                  wp_ref, wf_ref, wr_ref, wo_ref, o_ref):
    merged = (jax.nn.sigmoid(g0_ref[...]) * _dot(yp_ref[...], wp_ref[...])
              + jax.nn.sigmoid(g1_ref[...]) * _dot(yf_ref[...], wf_ref[...])
              + jax.nn.sigmoid(g2_ref[...]) * _dot(yr_ref[...], wr_ref[...]))
    o_ref[...] = x_ref[...] + _dot(merged.astype(BF16), wo_ref[...])


def _merge(y_pool, y_fox, y_ret, za, x, wp, wf, wr, wo, layer, *, gate_blk, tm=256):
    t, d = x.shape
    branch = lambda y: pl.BlockSpec((tm, y.shape[-1]), lambda m: (m, 0))
    gate = lambda i: pl.BlockSpec((tm, d), lambda m: (m, gate_blk + i))
    resident = lambda w: pl.BlockSpec((None,) + w.shape[1:], lambda m: (layer, 0, 0),
                                      pipeline_mode=pl.Buffered(1))
    return pl.pallas_call(
        _merge_kernel,
        grid=(t // tm,),
        in_specs=[branch(y_pool), branch(y_fox), branch(y_ret), gate(0), gate(1), gate(2),
                  pl.BlockSpec((tm, d), lambda m: (m, 0)),
                  resident(wp), resident(wf), resident(wr), resident(wo)],
        out_specs=pl.BlockSpec((tm, d), lambda m: (m, 0)),
        out_shape=jax.ShapeDtypeStruct((t, d), F32),
        compiler_params=_params("parallel"),
        name="merge",
    )(y_pool, y_fox, y_ret, za, za, za, x, wp, wf, wr, wo)


def kernel(x, ffn1_norm, ffn1_w13, ffn1_w2, mix_norm, w_in, forget_bias, pool_w, pool_scale,
           w_branch_pool, w_branch_fox, w_branch_ret, w_out, ffn2_norm, ffn2_w13, ffn2_w2,
           final_norm):
    b, s, d = x.shape
    depth = w_in.shape[0]
    pool_width = pool_scale.shape[-1]
    fox_heads = forget_bias.shape[-1]
    fox_width = w_branch_fox.shape[1]
    ret_width = w_branch_ret.shape[1]
    ret_heads = ret_width // HEAD_DIM
    assert fox_width == fox_heads * HEAD_DIM and pool_width == fox_width == ret_width
    assert d % fox_width == 0

    o = 0
    cuts = {}
    for name, width in (("pool", pool_width), ("fq", fox_width), ("fk", fox_width),
                        ("fv", fox_width), ("fl", fox_heads), ("rq", ret_width),
                        ("rk", ret_width), ("rv", ret_width), ("rg", ret_width),
                        ("gates", N_BRANCHES * d)):
        cuts[name] = slice(o, o + width)
        o += width
    assert o == w_in.shape[-1]
    take = lambda names: jnp.concatenate([w_in[..., cuts[n]] for n in names], axis=-1).astype(BF16)
    w_a = take(("pool", "rq", "rk", "rg", "gates"))
    w_b = take(("fq", "fk", "fv", "rv"))
    w_f = jnp.pad(w_in[..., cuts["fl"]], ((0, 0), (0, 0), (0, LANES - fox_heads))).astype(BF16)
    blk = lambda cols: cols // HEAD_DIM
    rq_blk, rk_blk, rg_blk = blk(pool_width), blk(pool_width + ret_width), blk(pool_width + 2 * ret_width)
    gate_blk = (pool_width + 3 * ret_width) // d
    rv_blk = blk(3 * fox_width)

    bias = jnp.pad(forget_bias, ((0, 0), (0, LANES - fox_heads)))[:, None, :]
    row = lambda a: a[:, None, :]
    w13_1, w2_1 = ffn1_w13.astype(BF16), ffn1_w2.astype(BF16)
    w13_2, w2_2 = ffn2_w13.astype(BF16), ffn2_w2.astype(BF16)
    pool_w_b = pool_w.astype(BF16)
    wp, wf, wr, wo = (w.astype(BF16) for w in (w_branch_pool, w_branch_fox, w_branch_ret, w_out))
    tables = _retention_tables(s, ret_heads)
    tq = 256

    xt = x.reshape(b * s, d)
    for l in range(depth):
        xt = _ffn(xt, row(ffn1_norm), w13_1, w2_1, l)
        za = _norm_matmul(xt, row(mix_norm), w_a, l, F32)
        zb = _norm_matmul(xt, row(mix_norm), w_b, l, BF16)
        fl = _norm_matmul(xt, row(mix_norm), w_f, l, F32)
        za3, zb3 = za.reshape(b, s, -1), zb.reshape(b, s, -1)
        cf, cft = _forget_cumsum(fl.reshape(b, s, LANES), bias, l, heads=fox_heads, tk=tq)
        y_fox = _fox(zb3, cf, cft, width=fox_width, tq=tq)
        y_ret = _retention(za3, zb3, tables, q_blk=rq_blk, k_blk=rk_blk, gate_blk=rg_blk,
                           v_blk=rv_blk, heads=ret_heads, tr=4 * CHUNK)
        y_pool = _pool(za3, pool_w_b, row(pool_scale), l, width=pool_width, ts=512)
        flat = lambda y: y.reshape(b * s, -1)
        xt = _merge(flat(y_pool), flat(y_fox), flat(y_ret), za, xt, wp, wf, wr, wo, l,
                    gate_blk=gate_blk)
        last = l == depth - 1
        xt = _ffn(xt, row(ffn2_norm), w13_2, w2_2, l,
                  final_gain=final_norm[None, :] if last else None)
    return xt.reshape(b, s, d)
```

```python
import functools

import jax
import jax.numpy as jnp
from jax import lax
from jax.experimental import pallas as pl
from jax.experimental.pallas import tpu as pltpu

F32 = jnp.float32
BF16 = jnp.bfloat16

RMS_EPS = 1e-6
GN_EPS = 1e-5
ROPE_BASE = 10000.0
POOL_WINDOWS = (2, 4, 8, 16)
POOL_HALO = 16
HEAD_DIM = 128
CHUNK = 128
N_BRANCHES = 3

LANES = 128
SUBLANES = 8
VMEM_LIMIT_BYTES = 56 * 1024 * 1024


def _params(*semantics):
    return pltpu.CompilerParams(dimension_semantics=semantics,
                                vmem_limit_bytes=VMEM_LIMIT_BYTES)


def _rms(x, g):
    return x * lax.rsqrt(jnp.mean(x * x, axis=-1, keepdims=True) + RMS_EPS) * g


def _silu(x):
    return x * jax.nn.sigmoid(x)


def _dot(a, b):
    return jnp.dot(a, b, preferred_element_type=F32)


def _dot_nt(a, b):
    return lax.dot_general(a, b, (((1,), (1,)), ((), ())), preferred_element_type=F32)


def _ffn_kernel(*refs, final_norm):
    if final_norm:
        x_ref, g_ref, w1_ref, w3_ref, w2_ref, gf_ref, o_ref, xn_ref = refs
    else:
        x_ref, g_ref, w1_ref, w3_ref, w2_ref, o_ref, xn_ref = refs
    f = pl.program_id(1)

    @pl.when(f == 0)
    def _():
        x = x_ref[...]
        xn_ref[...] = _rms(x, g_ref[...]).astype(BF16)
        o_ref[...] = x

    xn = xn_ref[...]
    h = (_silu(_dot(xn, w1_ref[...])) * _dot(xn, w3_ref[...])).astype(BF16)
    o_ref[...] += 0.5 * _dot(h, w2_ref[...])

    if final_norm:
        @pl.when(f == pl.num_programs(1) - 1)
        def _():
            o_ref[...] = _rms(o_ref[...], gf_ref[...])


def _ffn(x, norm, w13, w2, layer, final_gain=None, *, tm=1024, tf=512):
    t, d = x.shape
    d_ff = w2.shape[1]
    nf = d_ff // tf
    in_specs = [
        pl.BlockSpec((tm, d), lambda m, f: (m, 0)),
        pl.BlockSpec((None, 1, d), lambda m, f: (layer, 0, 0)),
        pl.BlockSpec((None, d, tf), lambda m, f: (layer, 0, f)),
        pl.BlockSpec((None, d, tf), lambda m, f: (layer, 0, f + nf)),
        pl.BlockSpec((None, tf, d), lambda m, f: (layer, f, 0)),
    ]
    args = [x, norm, w13, w13, w2]
    if final_gain is not None:
        in_specs.append(pl.BlockSpec((1, d), lambda m, f: (0, 0)))
        args.append(final_gain)
    return pl.pallas_call(
        functools.partial(_ffn_kernel, final_norm=final_gain is not None),
        grid=(t // tm, nf),
        in_specs=in_specs,
        out_specs=pl.BlockSpec((tm, d), lambda m, f: (m, 0)),
        out_shape=jax.ShapeDtypeStruct((t, d), F32),
        scratch_shapes=[pltpu.VMEM((tm, d), BF16)],
        compiler_params=_params("parallel", "arbitrary"),
        name="ffn",
    )(*args)


def _norm_matmul_kernel(x_ref, g_ref, w_ref, o_ref, xn_ref):
    @pl.when(pl.program_id(1) == 0)
    def _():
        xn_ref[...] = _rms(x_ref[...], g_ref[...]).astype(BF16)

    o_ref[...] = _dot(xn_ref[...], w_ref[...]).astype(o_ref.dtype)


def _norm_matmul(x, norm, w, layer, out_dtype, *, tm=1024, tn=1024):
    t, d = x.shape
    n = w.shape[-1]
    tn = min(tn, n)
    return pl.pallas_call(
        _norm_matmul_kernel,
        grid=(t // tm, n // tn),
        in_specs=[
            pl.BlockSpec((tm, d), lambda m, j: (m, 0)),
            pl.BlockSpec((None, 1, d), lambda m, j: (layer, 0, 0)),
            pl.BlockSpec((None, d, tn), lambda m, j: (layer, 0, j)),
        ],
        out_specs=pl.BlockSpec((tm, tn), lambda m, j: (m, j)),
        out_shape=jax.ShapeDtypeStruct((t, n), out_dtype),
        scratch_shapes=[pltpu.VMEM((tm, d), BF16)],
        compiler_params=_params("parallel", "arbitrary"),
        name="in_proj",
    )(x, norm, w)


def _forget_cumsum_kernel(fl_ref, b_ref, cf_ref, cft_ref, *, heads, tk):
    z = fl_ref[...] + b_ref[...]
    x = jnp.minimum(z, 0.0) - jnp.log1p(jnp.exp(-jnp.abs(z)))
    s = x.shape[0]
    row = lax.broadcasted_iota(jnp.int32, x.shape, 0)
    shift = 1
    while shift < s:
        if shift < SUBLANES:
            prev = jnp.where(row >= shift, pltpu.roll(x, shift, axis=0), 0.0)
        else:
            prev = jnp.concatenate([jnp.zeros((shift, x.shape[1]), F32), x[:s - shift]], axis=0)
        x = x + prev
        shift *= 2
    cf_ref[...] = x
    xt = x.T
    for j in range(s // tk):
        cft_ref[j] = xt[:heads, j * tk:(j + 1) * tk]


def _forget_cumsum(fl, bias, layer, *, heads, tk):
    b, s, w = fl.shape
    return pl.pallas_call(
        functools.partial(_forget_cumsum_kernel, heads=heads, tk=tk),
        grid=(b,),
        in_specs=[
            pl.BlockSpec((None, s, w), lambda i: (i, 0, 0)),
            pl.BlockSpec((None, 1, w), lambda i: (layer, 0, 0)),
        ],
        out_specs=[
            pl.BlockSpec((None, s, w), lambda i: (i, 0, 0)),
            pl.BlockSpec((None, s // tk, heads, tk), lambda i: (i, 0, 0, 0)),
        ],
        out_shape=[
            jax.ShapeDtypeStruct((b, s, w), F32),
            jax.ShapeDtypeStruct((b, s // tk, heads, tk), F32),
        ],
        compiler_params=_params("parallel"),
        name="forget_cumsum",
    )(fl, bias)


def _fox_kernel(q_ref, k_ref, v_ref, cf_ref, cft_ref, o_ref, fq_ref, m_ref, accl_ref, *, tq):
    i = pl.program_id(1)
    heads = q_ref.shape[-1] // HEAD_DIM
    log2e = 1.4426950408889634
    c1 = HEAD_DIM ** -0.5 * log2e
    causal = (lax.broadcasted_iota(jnp.int32, (tq, tq), 0)
              >= lax.broadcasted_iota(jnp.int32, (tq, tq), 1))
    ones = jnp.ones((tq, HEAD_DIM), BF16)
    twice = lambda a: jnp.concatenate([a, a], axis=1)

    cf = cf_ref[...] * log2e
    for h in range(heads):
        fq_ref[h] = jnp.broadcast_to(cf[:, h:h + 1], (tq, LANES))
    m_ref[...] = jnp.full(m_ref.shape, -jnp.inf, F32)
    accl_ref[...] = jnp.zeros(accl_ref.shape, F32)

    def block(j, masked):
        k0 = pl.multiple_of(j * tq, tq)
        for h in range(heads):
            lanes = slice(h * HEAD_DIM, (h + 1) * HEAD_DIM)
            kb = k_ref[pl.ds(k0, tq), lanes]
            v1 = jnp.concatenate([v_ref[pl.ds(k0, tq), lanes], ones], axis=1)
            fk = cft_ref[j, h:h + 1, :] * log2e
            t = _dot_nt(q_ref[:, lanes], kb) * c1 + twice(fq_ref[h]) - fk
            if masked:
                t = jnp.where(causal, t, -jnp.inf)
            m_prev = m_ref[h]
            m_new = jnp.maximum(m_prev, jnp.max(t, axis=-1, keepdims=True))
            p = jnp.exp2(t - twice(m_new))
            accl_ref[h] = twice(jnp.exp2(m_prev - m_new)) * accl_ref[h] + _dot(p.astype(BF16), v1)
            m_ref[h] = m_new

    lax.fori_loop(0, i, lambda j, c: (block(j, False), c)[1], 0)
    block(i, True)
    for h in range(heads):
        accl = accl_ref[h]
        o_ref[:, h * HEAD_DIM:(h + 1) * HEAD_DIM] = (
            accl[:, :HEAD_DIM] / accl[:, HEAD_DIM:]).astype(o_ref.dtype)


def _fox(zb, cf, cft, *, width, tq):
    b, s, _ = zb.shape
    heads = width // HEAD_DIM
    return pl.pallas_call(
        functools.partial(_fox_kernel, tq=tq),
        grid=(b, s // tq),
        in_specs=[
            pl.BlockSpec((None, tq, width), lambda bi, i: (bi, i, 0)),
            pl.BlockSpec((None, s, width), lambda bi, i: (bi, 0, 1)),
            pl.BlockSpec((None, s, width), lambda bi, i: (bi, 0, 2)),
            pl.BlockSpec((None, tq, LANES), lambda bi, i: (bi, i, 0)),
            pl.BlockSpec((None, s // tq, heads, tq), lambda bi, i: (bi, 0, 0, 0)),
        ],
        out_specs=pl.BlockSpec((None, tq, width), lambda bi, i: (bi, i, 0)),
        out_shape=jax.ShapeDtypeStruct((b, s, width), BF16),
        scratch_shapes=[pltpu.VMEM((heads, tq, LANES), F32),
                        pltpu.VMEM((heads, tq, LANES), F32),
                        pltpu.VMEM((heads, tq, 2 * HEAD_DIM), F32)],
        compiler_params=_params("parallel", "arbitrary"),
        name="fox",
    )(zb, zb, zb, cf, cft)


def _ret_kernel(q_ref, k_ref, v_ref, gate_ref, cos_ref, sin_ref, dm_ref, xi_ref, zeta_ref,
                cd_ref, o_ref, state_ref, *, chunks):
    heads = state_ref.shape[0]

    @pl.when(pl.program_id(1) == 0)
    def _():
        state_ref[...] = jnp.zeros_like(state_ref)

    for c in range(chunks):
        r = slice(c * CHUNK, (c + 1) * CHUNK)
        cos, sin = cos_ref[r, :], sin_ref[r, :]
        for h in range(heads):
            lanes = slice(h * HEAD_DIM, (h + 1) * HEAD_DIM)
            q, k = q_ref[r, lanes], k_ref[r, lanes]
            q = q * cos + pltpu.roll(q, HEAD_DIM // 2, axis=1) * sin
            k = (k * cos + pltpu.roll(k, HEAD_DIM // 2, axis=1) * sin) * (HEAD_DIM ** -0.5)
            v = v_ref[r, lanes]
            state = state_ref[h]
            scores = _dot_nt(q.astype(BF16), k.astype(BF16)) * dm_ref[h]
            o = _dot(scores.astype(BF16), v)
            o = o + _dot((q * xi_ref[h]).astype(BF16), state.astype(BF16))
            kz_t = (k * zeta_ref[h]).T.astype(BF16)
            state_ref[h] = cd_ref[h, 0:1, :] * state + _dot(kz_t, v)
            mu = jnp.mean(o, axis=-1, keepdims=True)
            var = jnp.mean(jnp.square(o - mu), axis=-1, keepdims=True)
            o = (o - mu) * lax.rsqrt(var + GN_EPS)
            o_ref[r, lanes] = (_silu(gate_ref[r, lanes]) * o).astype(o_ref.dtype)


def _retention(za, zb, tables, *, q_blk, k_blk, gate_blk, v_blk, heads, tr):
    b, s, _ = za.shape
    width = heads * HEAD_DIM
    cos, sin, dmat, xi, zeta, cdec = tables
    tok = lambda off: pl.BlockSpec((None, tr, width), lambda bi, i: (bi, i, off))
    whole = lambda a: pl.BlockSpec(a.shape, lambda bi, i: (0, 0, 0))
    pos = pl.BlockSpec((tr, HEAD_DIM), lambda bi, i: (i, 0))
    return pl.pallas_call(
        functools.partial(_ret_kernel, chunks=tr // CHUNK),
        grid=(b, s // tr),
        in_specs=[tok(q_blk), tok(k_blk), tok(v_blk), tok(gate_blk), pos, pos,
                  whole(dmat), whole(xi), whole(zeta), whole(cdec)],
        out_specs=pl.BlockSpec((None, tr, width), lambda bi, i: (bi, i, 0)),
        out_shape=jax.ShapeDtypeStruct((b, s, width), BF16),
        scratch_shapes=[pltpu.VMEM((heads, HEAD_DIM, HEAD_DIM), F32)],
        compiler_params=_params("parallel", "arbitrary"),
        name="retention",
    )(za, za, zb, za, cos, sin, dmat, xi, zeta, cdec)


def _retention_tables(s, heads):
    half = HEAD_DIM // 2
    inv_freq = ROPE_BASE ** (-jnp.arange(half, dtype=F32) / half)
    ang = jnp.arange(s, dtype=F32)[:, None] * inv_freq[None, :]
    cos = jnp.concatenate([jnp.cos(ang), jnp.cos(ang)], axis=-1)
    sin = jnp.concatenate([-jnp.sin(ang), jnp.sin(ang)], axis=-1)
    log_gamma = jnp.log1p(-jnp.power(2.0, -5.0 - jnp.arange(heads, dtype=F32)))
    pos = jnp.arange(CHUNK, dtype=F32)
    diff = pos[:, None] - pos[None, :]
    dmat = jnp.where(diff[None] >= 0,
                     jnp.exp(jnp.maximum(diff, 0.0)[None] * log_gamma[:, None, None]), 0.0)
    xi = jnp.exp((pos + 1.0)[None, :] * log_gamma[:, None])
    zeta = jnp.exp((CHUNK - 1.0 - pos)[None, :] * log_gamma[:, None])
    cdec = jnp.exp(CHUNK * log_gamma)
    wide = lambda a: jnp.broadcast_to(a[:, :, None], (heads, CHUNK, HEAD_DIM))
    return (cos, sin, dmat, wide(xi), wide(zeta),
            jnp.broadcast_to(cdec[:, None, None], (heads, SUBLANES, HEAD_DIM)))


def _pool_kernel(u_ref, w_ref, sc_ref, o_ref, ext_ref, *, ts):
    s = pl.program_id(1)
    gd = u_ref.shape[-1] // len(POOL_WINDOWS)

    @pl.when(s == 0)
    def _():
        ext_ref[0:POOL_HALO, :] = jnp.zeros((POOL_HALO, ext_ref.shape[1]), F32)

    @pl.when(s > 0)
    def _():
        ext_ref[0:POOL_HALO, :] = ext_ref[ts:ts + POOL_HALO, :]

    ext_ref[POOL_HALO:POOL_HALO + ts, :] = u_ref[...]
    t = s * ts + lax.broadcasted_iota(jnp.int32, (ts, 1), 0)
    for g, w in enumerate(POOL_WINDOWS):
        cols = slice(g * gd, (g + 1) * gd)
        u = ext_ref[POOL_HALO:POOL_HALO + ts, cols]
        win = u
        for k in range(1, w):
            win = win + ext_ref[POOL_HALO - k:POOL_HALO - k + ts, cols]
        cnt = jnp.minimum(t + 1, w).astype(F32)
        pooled = win / cnt - u
        mixed = _dot(pooled.astype(BF16), w_ref[g])
        o_ref[:, cols] = (mixed * sc_ref[:, cols]).astype(o_ref.dtype)


def _pool(za, pool_w, pool_scale, layer, *, width, ts):
    b, s, _ = za.shape
    groups, gd = pool_w.shape[1], pool_w.shape[2]
    return pl.pallas_call(
        functools.partial(_pool_kernel, ts=ts),
        grid=(b, s // ts),
        in_specs=[
            pl.BlockSpec((None, ts, width), lambda bi, i: (bi, i, 0)),
            pl.BlockSpec((None, groups, gd, gd), lambda bi, i: (layer, 0, 0, 0)),
            pl.BlockSpec((None, 1, width), lambda bi, i: (layer, 0, 0)),
        ],
        out_specs=pl.BlockSpec((None, ts, width), lambda bi, i: (bi, i, 0)),
        out_shape=jax.ShapeDtypeStruct((b, s, width), BF16),
        scratch_shapes=[pltpu.VMEM((POOL_HALO + ts, width), F32)],
        compiler_params=_params("parallel", "arbitrary"),
        name="pool",
    )(za, pool_w, pool_scale)


def _merge_kernel(yp_ref, yf_ref, yr_ref, g0_ref, g1_ref, g2_ref, x_ref,
                  wp_ref, wf_ref, wr_ref, wo_ref, o_ref):
    merged = (jax.nn.sigmoid(g0_ref[...]) * _dot(yp_ref[...], wp_ref[...])
              + jax.nn.sigmoid(g1_ref[...]) * _dot(yf_ref[...], wf_ref[...])
              + jax.nn.sigmoid(g2_ref[...]) * _dot(yr_ref[...], wr_ref[...]))
    o_ref[...] = x_ref[...] + _dot(merged.astype(BF16), wo_ref[...])


def _merge(y_pool, y_fox, y_ret, za, x, wp, wf, wr, wo, layer, *, gate_blk, tm=256):
    t, d = x.shape
    branch = lambda y: pl.BlockSpec((tm, y.shape[-1]), lambda m: (m, 0))
    gate = lambda i: pl.BlockSpec((tm, d), lambda m: (m, gate_blk + i))
    resident = lambda w: pl.BlockSpec((None,) + w.shape[1:], lambda m: (layer, 0, 0),
                                      pipeline_mode=pl.Buffered(1))
    return pl.pallas_call(
        _merge_kernel,
        grid=(t // tm,),
        in_specs=[branch(y_pool), branch(y_fox), branch(y_ret), gate(0), gate(1), gate(2),
                  pl.BlockSpec((tm, d), lambda m: (m, 0)),
                  resident(wp), resident(wf), resident(wr), resident(wo)],
        out_specs=pl.BlockSpec((tm, d), lambda m: (m, 0)),
        out_shape=jax.ShapeDtypeStruct((t, d), F32),
        compiler_params=_params("parallel"),
        name="merge",
    )(y_pool, y_fox, y_ret, za, za, za, x, wp, wf, wr, wo)


def kernel(x, ffn1_norm, ffn1_w13, ffn1_w2, mix_norm, w_in, forget_bias, pool_w, pool_scale,
           w_branch_pool, w_branch_fox, w_branch_ret, w_out, ffn2_norm, ffn2_w13, ffn2_w2,
           final_norm):
    b, s, d = x.shape
    depth = w_in.shape[0]
    pool_width = pool_scale.shape[-1]
    fox_heads = forget_bias.shape[-1]
    fox_width = w_branch_fox.shape[1]
    ret_width = w_branch_ret.shape[1]
    ret_heads = ret_width // HEAD_DIM
    assert fox_width == fox_heads * HEAD_DIM and pool_width == fox_width == ret_width
    assert d % fox_width == 0

    o = 0
    cuts = {}
    for name, width in (("pool", pool_width), ("fq", fox_width), ("fk", fox_width),
                        ("fv", fox_width), ("fl", fox_heads), ("rq", ret_width),
                        ("rk", ret_width), ("rv", ret_width), ("rg", ret_width),
                        ("gates", N_BRANCHES * d)):
        cuts[name] = slice(o, o + width)
        o += width
    assert o == w_in.shape[-1]
    w_in_b = w_in.astype(BF16)
    take = lambda names: jnp.concatenate([w_in_b[..., cuts[n]] for n in names], axis=-1)
    w_a = take(("pool", "rq", "rk", "rg", "gates"))
    w_b = take(("fq", "fk", "fv", "rv"))
    w_f = jnp.pad(w_in_b[..., cuts["fl"]], ((0, 0), (0, 0), (0, LANES - fox_heads)))
    rq_blk, rk_blk, rg_blk = 1, 2, 3
    gate_blk = (pool_width + 3 * ret_width) // d
    rv_blk = 3

    bias = jnp.pad(forget_bias, ((0, 0), (0, LANES - fox_heads)))[:, None, :]
    row = lambda a: a[:, None, :]
    w13_1, w2_1 = ffn1_w13.astype(BF16), ffn1_w2.astype(BF16)
    w13_2, w2_2 = ffn2_w13.astype(BF16), ffn2_w2.astype(BF16)
    pool_w_b = pool_w.astype(BF16)
    wp, wf, wr, wo = (w.astype(BF16) for w in (w_branch_pool, w_branch_fox, w_branch_ret, w_out))
    tables = _retention_tables(s, ret_heads)
    tq = 256

    xt = x.reshape(b * s, d)
    for l in range(depth):
        xt = _ffn(xt, row(ffn1_norm), w13_1, w2_1, l)
        za = _norm_matmul(xt, row(mix_norm), w_a, l, F32)
        zb = _norm_matmul(xt, row(mix_norm), w_b, l, BF16)
        fl = _norm_matmul(xt, row(mix_norm), w_f, l, F32)
        za3, zb3 = za.reshape(b, s, -1), zb.reshape(b, s, -1)
        cf, cft = _forget_cumsum(fl.reshape(b, s, LANES), bias, l, heads=fox_heads, tk=tq)
        y_fox = _fox(zb3, cf, cft, width=fox_width, tq=tq)
        y_ret = _retention(za3, zb3, tables, q_blk=rq_blk, k_blk=rk_blk, gate_blk=rg_blk,
                           v_blk=rv_blk, heads=ret_heads, tr=4 * CHUNK)
        y_pool = _pool(za3, pool_w_b, row(pool_scale), l, width=pool_width, ts=512)
        flat = lambda y: y.reshape(b * s, -1)
        xt = _merge(flat(y_pool), flat(y_fox), flat(y_ret), za, xt, wp, wf, wr, wo, l,
                    gate_blk=gate_blk)
        last = l == depth - 1
        xt = _ffn(xt, row(ffn2_norm), w13_2, w2_2, l,
                  final_gain=final_norm[None, :] if last else None)
    return xt.reshape(b, s, d)
```

```python
import functools

import jax
import jax.numpy as jnp
from jax import lax
from jax.experimental import pallas as pl
from jax.experimental.pallas import tpu as pltpu

F32 = jnp.float32
BF16 = jnp.bfloat16

RMS_EPS = 1e-6
GN_EPS = 1e-5
ROPE_BASE = 10000.0
POOL_WINDOWS = (2, 4, 8, 16)
POOL_HALO = 16
HEAD_DIM = 128
CHUNK = 128
N_BRANCHES = 3

LANES = 128
SUBLANES = 8
VMEM_LIMIT_BYTES = 60 * 1024 * 1024


def _params(*semantics):
    return pltpu.CompilerParams(dimension_semantics=semantics,
                                vmem_limit_bytes=VMEM_LIMIT_BYTES)


def _rms(x, g):
    return x * lax.rsqrt(jnp.mean(x * x, axis=-1, keepdims=True) + RMS_EPS) * g


def _silu(x):
    return x * jax.nn.sigmoid(x)


def _dot(a, b):
    return jnp.dot(a, b, preferred_element_type=F32)


def _dot_nt(a, b):
    return lax.dot_general(a, b, (((1,), (1,)), ((), ())), preferred_element_type=F32)


def _ffn_kernel(*refs, final_norm, convert_next):
    refs = list(refs)
    x_ref, g_ref, w1_ref, w3_ref, w2_ref = refs[:5]
    del refs[:5]
    gf_ref = refs.pop(0) if final_norm else None
    next_w13_ref, next_w2_ref = (refs.pop(0), refs.pop(0)) if convert_next else (None, None)
    o_ref = refs.pop(0)
    cast_w13_ref, cast_w2_ref = (refs.pop(0), refs.pop(0)) if convert_next else (None, None)
    (xn_ref,) = refs
    f = pl.program_id(1)

    @pl.when(f == 0)
    def _():
        x = x_ref[...]
        xn_ref[...] = _rms(x, g_ref[...]).astype(BF16)
        o_ref[...] = x

    xn = xn_ref[...]
    h = (_silu(_dot(xn, w1_ref[...])) * _dot(xn, w3_ref[...])).astype(BF16)
    o_ref[...] += 0.5 * _dot(h, w2_ref[...])

    if final_norm:
        @pl.when(f == pl.num_programs(1) - 1)
        def _():
            o_ref[...] = _rms(o_ref[...], gf_ref[...])

    if convert_next:
        cast_w13_ref[...] = next_w13_ref[...].astype(BF16)
        cast_w2_ref[...] = next_w2_ref[...].astype(BF16)


def _ffn(x, gain, w13, w2, *, final_gain=None, next_weights=None, tm=1024, tf=512):
    t, d = x.shape
    d_ff = w2.shape[0]
    gm, nf = t // tm, d_ff // tf
    in_specs = [
        pl.BlockSpec((tm, d), lambda m, f: (m, 0)),
        pl.BlockSpec((1, d), lambda m, f: (0, 0)),
        pl.BlockSpec((d, tf), lambda m, f: (0, f)),
        pl.BlockSpec((d, tf), lambda m, f: (0, f + nf)),
        pl.BlockSpec((tf, d), lambda m, f: (f, 0)),
    ]
    args = [x, gain, w13, w13, w2]
    out_specs = [pl.BlockSpec((tm, d), lambda m, f: (m, 0))]
    out_shape = [jax.ShapeDtypeStruct((t, d), F32)]
    if final_gain is not None:
        in_specs.append(pl.BlockSpec((1, d), lambda m, f: (0, 0)))
        args.append(final_gain)
    if next_weights is not None:
        n13, n2, nl = next_weights
        b13 = (d // gm, 2 * d_ff // nf)
        b2 = (d_ff // nf, d // gm)
        in_specs += [pl.BlockSpec((None,) + b13, lambda m, f: (nl, m, f)),
                     pl.BlockSpec((None,) + b2, lambda m, f: (nl, f, m))]
        args += [n13, n2]
        out_specs += [pl.BlockSpec(b13, lambda m, f: (m, f)), pl.BlockSpec(b2, lambda m, f: (f, m))]
        out_shape += [jax.ShapeDtypeStruct(n13.shape[1:], BF16),
                      jax.ShapeDtypeStruct(n2.shape[1:], BF16)]
    outs = pl.pallas_call(
        functools.partial(_ffn_kernel, final_norm=final_gain is not None,
                          convert_next=next_weights is not None),
        grid=(gm, nf),
        in_specs=in_specs,
        out_specs=out_specs,
        out_shape=out_shape,
        scratch_shapes=[pltpu.VMEM((tm, d), BF16)],
        compiler_params=_params("parallel", "arbitrary"),
        name="ffn",
    )(*args)
    return outs if next_weights is not None else (outs[0], None, None)


def _regroup_layout(n_src, gate_start, gate_cols, n_chunks):
    body = n_src - gate_cols
    chunk = -(-(body + LANES) // (n_chunks * LANES)) * LANES
    return body, n_chunks * chunk, chunk


def _regroup_tile(main_ref, next_ref, gate_ref, out_ref, j, *, n_src, gate_start, gate_cols,
                  n_chunks):
    body, _, chunk = _regroup_layout(n_src, gate_start, gate_cols, n_chunks)
    assert body % LANES == 0 and gate_start % LANES == 0

    def shifted():
        ext = jnp.concatenate([main_ref[...], next_ref[...]], axis=1)
        return pltpu.roll(ext, ext.shape[1] - gate_cols, axis=1)[:, :chunk]

    for c in range(n_chunks):
        lo, hi = c * chunk, (c + 1) * chunk

        @pl.when(j == c)
        def _():
            pieces = []
            if lo < gate_start:
                pieces.append(main_ref[:, :min(hi, gate_start) - lo])
            if max(lo, gate_start) < min(hi, body):
                pieces.append(shifted()[:, max(lo, gate_start) - lo:min(hi, body) - lo])
            if lo <= body < hi:
                lane = lax.broadcasted_iota(jnp.int32, gate_ref.shape, 1)
                pieces.append(jnp.where(lane < gate_cols, gate_ref[...], 0.0))
            done = sum(p.shape[1] for p in pieces)
            if done < chunk:
                pieces.append(jnp.zeros((out_ref.shape[0], chunk - done), F32))
            out_ref[...] = jnp.concatenate(pieces, axis=1).astype(out_ref.dtype)


def _in_proj_kernel(blocks_ref, *refs, n_f32, regroup):
    del blocks_ref
    refs = list(refs)
    x_ref, g_ref, w_ref, wg_ref = refs[:4]
    del refs[:4]
    main_ref, next_ref, gate_ref = (refs.pop(0), refs.pop(0), refs.pop(0)) if regroup else (None,) * 3
    za_ref, zb_ref, fl_ref = refs[:3]
    del refs[:3]
    cast_ref = refs.pop(0) if regroup else None
    (xn_ref,) = refs
    j = pl.program_id(1)

    @pl.when(j == 0)
    def _():
        xn_ref[...] = _rms(x_ref[...], g_ref[...]).astype(BF16)
        fl_ref[...] = _dot(xn_ref[...], wg_ref[...])

    @pl.when(j < n_f32)
    def _():
        za_ref[...] = _dot(xn_ref[...], w_ref[...])

    @pl.when(j >= n_f32)
    def _():
        zb_ref[...] = _dot(xn_ref[...], w_ref[...]).astype(zb_ref.dtype)

    if regroup:
        _regroup_tile(main_ref, next_ref, gate_ref, cast_ref, j, **regroup)


def _in_proj(x, gain, w, blocks_f32, blocks_bf16, gate_block, *, tn, regroup_next=None, tm=1024):
    t, d = x.shape
    n_f32, n_bf16 = len(blocks_f32), len(blocks_bf16)
    gm, gn = t // tm, n_f32 + n_bf16
    table = jnp.asarray(list(blocks_f32) + list(blocks_bf16), jnp.int32)
    in_specs = [
        pl.BlockSpec((tm, d), lambda m, j, tab: (m, 0)),
        pl.BlockSpec((1, d), lambda m, j, tab: (0, 0)),
        pl.BlockSpec((d, tn), lambda m, j, tab: (0, tab[j])),
        pl.BlockSpec((d, LANES), lambda m, j, tab: (0, gate_block)),
    ]
    args = [x, gain, w, w]
    out_specs = [
        pl.BlockSpec((tm, tn), lambda m, j, tab: (m, jnp.minimum(j, n_f32 - 1))),
        pl.BlockSpec((tm, tn), lambda m, j, tab: (m, jnp.maximum(j - n_f32, 0))),
        pl.BlockSpec((tm, LANES), lambda m, j, tab: (m, 0)),
    ]
    out_shape = [jax.ShapeDtypeStruct((t, n_f32 * tn), F32),
                 jax.ShapeDtypeStruct((t, n_bf16 * tn), BF16),
                 jax.ShapeDtypeStruct((t, LANES), F32)]
    regroup = None
    if regroup_next is not None:
        src, nl, gate_start, gate_cols = regroup_next
        n_src = src.shape[-1]
        regroup = dict(n_src=n_src, gate_start=gate_start, gate_cols=gate_cols, n_chunks=gn)
        _, width, chunk = _regroup_layout(**regroup)
        rows = d // gm
        last_tile = (n_src - 1) // LANES
        in_specs += [
            pl.BlockSpec((None, rows, chunk),
                         lambda m, j, tab: (nl, m, jnp.minimum(j, (n_src - 1) // chunk))),
            pl.BlockSpec((None, rows, LANES),
                         lambda m, j, tab: (nl, m, jnp.minimum((j + 1) * (chunk // LANES), last_tile))),
            pl.BlockSpec((None, rows, LANES), lambda m, j, tab: (nl, m, gate_start // LANES)),
        ]
        args += [src, src, src]
        out_specs.append(pl.BlockSpec((rows, chunk), lambda m, j, tab: (m, j)))
        out_shape.append(jax.ShapeDtypeStruct((d, width), BF16))
    outs = pl.pallas_call(
        functools.partial(_in_proj_kernel, n_f32=n_f32, regroup=regroup),
        grid_spec=pltpu.PrefetchScalarGridSpec(
            num_scalar_prefetch=1, grid=(gm, gn), in_specs=in_specs, out_specs=out_specs,
            scratch_shapes=[pltpu.VMEM((tm, d), BF16)]),
        out_shape=out_shape,
        compiler_params=_params("parallel", "arbitrary"),
        name="in_proj",
    )(table, *args)
    return tuple(outs) if regroup_next is not None else tuple(outs) + (None,)


def _forget_cumsum_kernel(fl_ref, b_ref, cf_ref, cft_ref, *, heads, tk):
    z = fl_ref[...] + b_ref[...]
    x = jnp.minimum(z, 0.0) - jnp.log1p(jnp.exp(-jnp.abs(z)))
    s = x.shape[0]
    row = lax.broadcasted_iota(jnp.int32, x.shape, 0)
    shift = 1
    while shift < s:
        if shift < SUBLANES:
            prev = jnp.where(row >= shift, pltpu.roll(x, shift, axis=0), 0.0)
        else:
            prev = jnp.concatenate([jnp.zeros((shift, x.shape[1]), F32), x[:s - shift]], axis=0)
        x = x + prev
        shift *= 2
    cf_ref[...] = x
    xt = x.T
    for j in range(s // tk):
        cft_ref[j] = xt[:heads, j * tk:(j + 1) * tk]


def _forget_cumsum(fl, bias, layer, *, heads, tk):
    b, s, w = fl.shape
    return pl.pallas_call(
        functools.partial(_forget_cumsum_kernel, heads=heads, tk=tk),
        grid=(b,),
        in_specs=[
            pl.BlockSpec((None, s, w), lambda i: (i, 0, 0)),
            pl.BlockSpec((None, 1, w), lambda i: (layer, 0, 0)),
        ],
        out_specs=[
            pl.BlockSpec((None, s, w), lambda i: (i, 0, 0)),
            pl.BlockSpec((None, s // tk, heads, tk), lambda i: (i, 0, 0, 0)),
        ],
        out_shape=[
            jax.ShapeDtypeStruct((b, s, w), F32),
            jax.ShapeDtypeStruct((b, s // tk, heads, tk), F32),
        ],
        compiler_params=_params("parallel"),
        name="forget_cumsum",
    )(fl, bias)


def _fox_kernel(q_ref, k_ref, v_ref, cf_ref, cft_ref, o_ref, fq_ref, m_ref, accl_ref, *, tq):
    i = pl.program_id(1)
    heads = q_ref.shape[-1] // HEAD_DIM
    log2e = 1.4426950408889634
    c1 = HEAD_DIM ** -0.5 * log2e
    causal = (lax.broadcasted_iota(jnp.int32, (tq, tq), 0)
              >= lax.broadcasted_iota(jnp.int32, (tq, tq), 1))
    ones = jnp.ones((tq, HEAD_DIM), BF16)
    twice = lambda a: jnp.concatenate([a, a], axis=1)

    cf = cf_ref[...] * log2e
    for h in range(heads):
        fq_ref[h] = jnp.broadcast_to(cf[:, h:h + 1], (tq, LANES))
    m_ref[...] = jnp.full(m_ref.shape, -jnp.inf, F32)
    accl_ref[...] = jnp.zeros(accl_ref.shape, F32)

    def block(j, masked):
        k0 = pl.multiple_of(j * tq, tq)
        for h in range(heads):
            lanes = slice(h * HEAD_DIM, (h + 1) * HEAD_DIM)
            kb = k_ref[pl.ds(k0, tq), lanes]
            v1 = jnp.concatenate([v_ref[pl.ds(k0, tq), lanes], ones], axis=1)
            fk = cft_ref[j, h:h + 1, :] * log2e
            t = _dot_nt(q_ref[:, lanes], kb) * c1 + twice(fq_ref[h]) - fk
            if masked:
                t = jnp.where(causal, t, -jnp.inf)
            m_prev = m_ref[h]
            m_new = jnp.maximum(m_prev, jnp.max(t, axis=-1, keepdims=True))
            p = jnp.exp2(t - twice(m_new))
            accl_ref[h] = twice(jnp.exp2(m_prev - m_new)) * accl_ref[h] + _dot(p.astype(BF16), v1)
            m_ref[h] = m_new

    lax.fori_loop(0, i, lambda j, c: (block(j, False), c)[1], 0)
    block(i, True)
    for h in range(heads):
        accl = accl_ref[h]
        o_ref[:, h * HEAD_DIM:(h + 1) * HEAD_DIM] = (
            accl[:, :HEAD_DIM] / accl[:, HEAD_DIM:]).astype(o_ref.dtype)


def _fox(zb, cf, cft, *, width, tq):
    b, s, _ = zb.shape
    heads = width // HEAD_DIM
    return pl.pallas_call(
        functools.partial(_fox_kernel, tq=tq),
        grid=(b, s // tq),
        in_specs=[
            pl.BlockSpec((None, tq, width), lambda bi, i: (bi, i, 0)),
            pl.BlockSpec((None, s, width), lambda bi, i: (bi, 0, 1)),
            pl.BlockSpec((None, s, width), lambda bi, i: (bi, 0, 2)),
            pl.BlockSpec((None, tq, LANES), lambda bi, i: (bi, i, 0)),
            pl.BlockSpec((None, s // tq, heads, tq), lambda bi, i: (bi, 0, 0, 0)),
        ],
        out_specs=pl.BlockSpec((None, tq, width), lambda bi, i: (bi, i, 0)),
        out_shape=jax.ShapeDtypeStruct((b, s, width), BF16),
        scratch_shapes=[pltpu.VMEM((heads, tq, LANES), F32),
                        pltpu.VMEM((heads, tq, LANES), F32),
                        pltpu.VMEM((heads, tq, 2 * HEAD_DIM), F32)],
        compiler_params=_params("parallel", "arbitrary"),
        name="fox",
    )(zb, zb, zb, cf, cft)


def _ret_kernel(q_ref, k_ref, v_ref, gate_ref, cos_ref, sin_ref, dm_ref, xi_ref, zeta_ref,
                cd_ref, o_ref, state_ref, *, chunks):
    heads = state_ref.shape[0]

    @pl.when(pl.program_id(1) == 0)
    def _():
        state_ref[...] = jnp.zeros_like(state_ref)

    for c in range(chunks):
        r = slice(c * CHUNK, (c + 1) * CHUNK)
        cos, sin = cos_ref[r, :], sin_ref[r, :]
        for h in range(heads):
            lanes = slice(h * HEAD_DIM, (h + 1) * HEAD_DIM)
            q, k = q_ref[r, lanes], k_ref[r, lanes]
            q = q * cos + pltpu.roll(q, HEAD_DIM // 2, axis=1) * sin
            k = (k * cos + pltpu.roll(k, HEAD_DIM // 2, axis=1) * sin) * (HEAD_DIM ** -0.5)
            v = v_ref[r, lanes]
            state = state_ref[h]
            scores = _dot_nt(q.astype(BF16), k.astype(BF16)) * dm_ref[h]
            o = _dot(scores.astype(BF16), v)
            o = o + _dot((q * xi_ref[h]).astype(BF16), state.astype(BF16))
            kz_t = (k * zeta_ref[h]).T.astype(BF16)
            state_ref[h] = cd_ref[h, 0:1, :] * state + _dot(kz_t, v)
            mu = jnp.mean(o, axis=-1, keepdims=True)
            var = jnp.mean(jnp.square(o - mu), axis=-1, keepdims=True)
            o = (o - mu) * lax.rsqrt(var + GN_EPS)
            o_ref[r, lanes] = (_silu(gate_ref[r, lanes]) * o).astype(o_ref.dtype)


def _retention(za, zb, tables, *, q_blk, k_blk, gate_blk, v_blk, heads, tr):
    b, s, _ = za.shape
    width = heads * HEAD_DIM
    cos, sin, dmat, xi, zeta, cdec = tables
    tok = lambda off: pl.BlockSpec((None, tr, width), lambda bi, i: (bi, i, off))
    whole = lambda a: pl.BlockSpec(a.shape, lambda bi, i: (0, 0, 0))
    pos = pl.BlockSpec((tr, HEAD_DIM), lambda bi, i: (i, 0))
    return pl.pallas_call(
        functools.partial(_ret_kernel, chunks=tr // CHUNK),
        grid=(b, s // tr),
        in_specs=[tok(q_blk), tok(k_blk), tok(v_blk), tok(gate_blk), pos, pos,
                  whole(dmat), whole(xi), whole(zeta), whole(cdec)],
        out_specs=pl.BlockSpec((None, tr, width), lambda bi, i: (bi, i, 0)),
        out_shape=jax.ShapeDtypeStruct((b, s, width), BF16),
        scratch_shapes=[pltpu.VMEM((heads, HEAD_DIM, HEAD_DIM), F32)],
        compiler_params=_params("parallel", "arbitrary"),
        name="retention",
    )(za, za, zb, za, cos, sin, dmat, xi, zeta, cdec)


def _retention_tables(s, heads):
    half = HEAD_DIM // 2
    inv_freq = ROPE_BASE ** (-jnp.arange(half, dtype=F32) / half)
    ang = jnp.arange(s, dtype=F32)[:, None] * inv_freq[None, :]
    cos = jnp.concatenate([jnp.cos(ang), jnp.cos(ang)], axis=-1)
    sin = jnp.concatenate([-jnp.sin(ang), jnp.sin(ang)], axis=-1)
    log_gamma = jnp.log1p(-jnp.power(2.0, -5.0 - jnp.arange(heads, dtype=F32)))
    pos = jnp.arange(CHUNK, dtype=F32)
    diff = pos[:, None] - pos[None, :]
    dmat = jnp.where(diff[None] >= 0,
                     jnp.exp(jnp.maximum(diff, 0.0)[None] * log_gamma[:, None, None]), 0.0)
    xi = jnp.exp((pos + 1.0)[None, :] * log_gamma[:, None])
    zeta = jnp.exp((CHUNK - 1.0 - pos)[None, :] * log_gamma[:, None])
    cdec = jnp.exp(CHUNK * log_gamma)
    wide = lambda a: jnp.broadcast_to(a[:, :, None], (heads, CHUNK, HEAD_DIM))
    return (cos, sin, dmat, wide(xi), wide(zeta),
            jnp.broadcast_to(cdec[:, None, None], (heads, SUBLANES, HEAD_DIM)))


def _pool_kernel(u_ref, w_ref, sc_ref, o_ref, ext_ref, *, ts):
    s = pl.program_id(1)
    gd = u_ref.shape[-1] // len(POOL_WINDOWS)

    @pl.when(s == 0)
    def _():
        ext_ref[0:POOL_HALO, :] = jnp.zeros((POOL_HALO, ext_ref.shape[1]), F32)

    @pl.when(s > 0)
    def _():
        ext_ref[0:POOL_HALO, :] = ext_ref[ts:ts + POOL_HALO, :]

    ext_ref[POOL_HALO:POOL_HALO + ts, :] = u_ref[...]
    t = s * ts + lax.broadcasted_iota(jnp.int32, (ts, 1), 0)
    for g, w in enumerate(POOL_WINDOWS):
        cols = slice(g * gd, (g + 1) * gd)
        u = ext_ref[POOL_HALO:POOL_HALO + ts, cols]
        win = u
        for k in range(1, w):
            win = win + ext_ref[POOL_HALO - k:POOL_HALO - k + ts, cols]
        cnt = jnp.minimum(t + 1, w).astype(F32)
        pooled = win / cnt - u
        mixed = _dot(pooled.astype(BF16), w_ref[g])
        o_ref[:, cols] = (mixed * sc_ref[:, cols]).astype(o_ref.dtype)


def _pool(za, pool_w, pool_scale, layer, *, width, ts):
    b, s, _ = za.shape
    groups, gd = pool_w.shape[1], pool_w.shape[2]
    return pl.pallas_call(
        functools.partial(_pool_kernel, ts=ts),
        grid=(b, s // ts),
        in_specs=[
            pl.BlockSpec((None, ts, width), lambda bi, i: (bi, i, 0)),
            pl.BlockSpec((None, groups, gd, gd), lambda bi, i: (layer, 0, 0, 0)),
            pl.BlockSpec((None, 1, width), lambda bi, i: (layer, 0, 0)),
        ],
        out_specs=pl.BlockSpec((None, ts, width), lambda bi, i: (bi, i, 0)),
        out_shape=jax.ShapeDtypeStruct((b, s, width), BF16),
        scratch_shapes=[pltpu.VMEM((POOL_HALO + ts, width), F32)],
        compiler_params=_params("parallel", "arbitrary"),
        name="pool",
    )(za, pool_w, pool_scale)


def _merge_kernel(yp_ref, yf_ref, yr_ref, g0_ref, g1_ref, g2_ref, x_ref,
                  wp_ref, wf_ref, wr_ref, wo_ref, o_ref):
    merged = (jax.nn.sigmoid(g0_ref[...]) * _dot(yp_ref[...], wp_ref[...])
              + jax.nn.sigmoid(g1_ref[...]) * _dot(yf_ref[...], wf_ref[...])
              + jax.nn.sigmoid(g2_ref[...]) * _dot(yr_ref[...], wr_ref[...]))
    o_ref[...] = x_ref[...] + _dot(merged.astype(BF16), wo_ref[...])


def _merge(y_pool, y_fox, y_ret, za, x, wp, wf, wr, wo, layer, *, gate_blk, tm=256):
    t, d = x.shape
    branch = lambda y: pl.BlockSpec((tm, y.shape[-1]), lambda m: (m, 0))
    gate = lambda i: pl.BlockSpec((tm, d), lambda m: (m, gate_blk + i))
    resident = lambda w: pl.BlockSpec((None,) + w.shape[1:], lambda m: (layer, 0, 0),
                                      pipeline_mode=pl.Buffered(1))
    return pl.pallas_call(
        _merge_kernel,
        grid=(t // tm,),
        in_specs=[branch(y_pool), branch(y_fox), branch(y_ret), gate(0), gate(1), gate(2),
                  pl.BlockSpec((tm, d), lambda m: (m, 0)),
                  resident(wp), resident(wf), resident(wr), resident(wo)],
        out_specs=pl.BlockSpec((tm, d), lambda m: (m, 0)),
        out_shape=jax.ShapeDtypeStruct((t, d), F32),
        compiler_params=_params("parallel"),
        name="merge",
    )(y_pool, y_fox, y_ret, za, za, za, x, wp, wf, wr, wo)


def kernel(x, ffn1_norm, ffn1_w13, ffn1_w2, mix_norm, w_in, forget_bias, pool_w, pool_scale,
           w_branch_pool, w_branch_fox, w_branch_ret, w_out, ffn2_norm, ffn2_w13, ffn2_w2,
           final_norm):
    b, s, d = x.shape
    depth = w_in.shape[0]
    pool_width = pool_scale.shape[-1]
    fox_heads = forget_bias.shape[-1]
    fox_width = w_branch_fox.shape[1]
    ret_width = w_branch_ret.shape[1]
    ret_heads = ret_width // HEAD_DIM
    assert fox_width == fox_heads * HEAD_DIM and pool_width == fox_width == ret_width
    assert d % fox_width == 0

    o = 0
    cuts = {}
    for name, width in (("pool", pool_width), ("fq", fox_width), ("fk", fox_width),
                        ("fv", fox_width), ("fl", fox_heads), ("rq", ret_width),
                        ("rk", ret_width), ("rv", ret_width), ("rg", ret_width),
                        ("gates", N_BRANCHES * d)):
        cuts[name] = slice(o, o + width)
        o += width
    assert o == w_in.shape[-1]
    gate_start, gate_cols = cuts["fl"].start, fox_heads
    tn = pool_width
    moved = lambda name: cuts[name].start - (gate_cols if cuts[name].start > gate_start else 0)
    blocks_of = lambda name: [moved(name) // tn + i for i in range((cuts[name].stop - cuts[name].start) // tn)]
    assert all(moved(n) % tn == 0 for n in cuts if n != "fl")
    blocks_a = sum((blocks_of(n) for n in ("pool", "rq", "rk", "rg", "gates")), [])
    blocks_b = sum((blocks_of(n) for n in ("fq", "fk", "fv", "rv")), [])
    body, width, _ = _regroup_layout(w_in.shape[-1], gate_start, gate_cols,
                                     len(blocks_a) + len(blocks_b))
    src0 = w_in[0]
    w_r = jnp.concatenate([src0[:, :gate_start], src0[:, gate_start + gate_cols:],
                           src0[:, gate_start:gate_start + gate_cols],
                           jnp.zeros((d, width - w_in.shape[-1]), F32)], axis=1).astype(BF16)
    rq_blk, rk_blk, rg_blk = 1, 2, 3
    gate_blk = (pool_width + 3 * ret_width) // d
    rv_blk = 3

    bias = jnp.pad(forget_bias, ((0, 0), (0, LANES - fox_heads)))[:, None, :]
    row = lambda a: a[:, None, :]
    pool_w_b = pool_w.astype(BF16)
    wp, wf, wr, wo = (w.astype(BF16) for w in (w_branch_pool, w_branch_fox, w_branch_ret, w_out))
    tables = _retention_tables(s, ret_heads)
    tq = 256

    halves = [(norm, w13, w2, l) for l in range(depth)
              for norm, w13, w2 in ((ffn1_norm, ffn1_w13, ffn1_w2), (ffn2_norm, ffn2_w13, ffn2_w2))]

    def half_step(xt, i, w13_b, w2_b):
        norm, _, _, l = halves[i]
        last = i == len(halves) - 1
        nxt = None if last else (halves[i + 1][1], halves[i + 1][2], halves[i + 1][3])
        return _ffn(xt, norm[l][None, :], w13_b, w2_b, next_weights=nxt,
                    final_gain=final_norm[None, :] if last else None)

    xt = x.reshape(b * s, d)
    w13_b, w2_b = ffn1_w13[0].astype(BF16), ffn1_w2[0].astype(BF16)
    for l in range(depth):
        xt, w13_b, w2_b = half_step(xt, 2 * l, w13_b, w2_b)
        gain = mix_norm[l][None, :]
        nxt = (w_in, l + 1, gate_start, gate_cols) if l + 1 < depth else None
        za, zb, fl, w_r_next = _in_proj(xt, gain, w_r, blocks_a, blocks_b, body // LANES, tn=tn,
                                        regroup_next=nxt)
        za3, zb3 = za.reshape(b, s, -1), zb.reshape(b, s, -1)
        cf, cft = _forget_cumsum(fl.reshape(b, s, LANES), bias, l, heads=fox_heads, tk=tq)
        y_fox = _fox(zb3, cf, cft, width=fox_width, tq=tq)
        y_ret = _retention(za3, zb3, tables, q_blk=rq_blk, k_blk=rk_blk, gate_blk=rg_blk,
                           v_blk=rv_blk, heads=ret_heads, tr=4 * CHUNK)
        y_pool = _pool(za3, pool_w_b, row(pool_scale), l, width=pool_width, ts=512)
        flat = lambda y: y.reshape(b * s, -1)
        xt = _merge(flat(y_pool), flat(y_fox), flat(y_ret), za, xt, wp, wf, wr, wo, l,
                    gate_blk=gate_blk)
        xt, w13_b, w2_b = half_step(xt, 2 * l + 1, w13_b, w2_b)
        w_r = w_r_next
    return xt.reshape(b, s, d)
```

```python
import functools

import jax
import jax.numpy as jnp
from jax import lax
from jax.experimental import pallas as pl
from jax.experimental.pallas import tpu as pltpu

F32 = jnp.float32
BF16 = jnp.bfloat16

RMS_EPS = 1e-6
GN_EPS = 1e-5
ROPE_BASE = 10000.0
POOL_WINDOWS = (2, 4, 8, 16)
POOL_HALO = 16
HEAD_DIM = 128
CHUNK = 128
N_BRANCHES = 3

LANES = 128
SUBLANES = 8
VMEM_LIMIT_BYTES = 60 * 1024 * 1024


def _params(*semantics):
    return pltpu.CompilerParams(dimension_semantics=semantics,
                                vmem_limit_bytes=VMEM_LIMIT_BYTES)


def _rms(x, g):
    return x * lax.rsqrt(jnp.mean(x * x, axis=-1, keepdims=True) + RMS_EPS) * g


def _silu(x):
    return x * jax.nn.sigmoid(x)


def _dot(a, b):
    return jnp.dot(a, b, preferred_element_type=F32)


def _dot_nt(a, b):
    return lax.dot_general(a, b, (((1,), (1,)), ((), ())), preferred_element_type=F32)


def _ffn_kernel(*refs, final_norm, convert_next):
    refs = list(refs)
    x_ref, g_ref, w1_ref, w3_ref, w2_ref = refs[:5]
    del refs[:5]
    gf_ref = refs.pop(0) if final_norm else None
    next_w13_ref, next_w2_ref = (refs.pop(0), refs.pop(0)) if convert_next else (None, None)
    o_ref = refs.pop(0)
    cast_w13_ref, cast_w2_ref = (refs.pop(0), refs.pop(0)) if convert_next else (None, None)
    (xn_ref,) = refs
    f = pl.program_id(1)

    @pl.when(f == 0)
    def _():
        x = x_ref[...]
        xn_ref[...] = _rms(x, g_ref[...]).astype(BF16)
        o_ref[...] = x

    xn = xn_ref[...]
    h = (_silu(_dot(xn, w1_ref[...])) * _dot(xn, w3_ref[...])).astype(BF16)
    o_ref[...] += 0.5 * _dot(h, w2_ref[...])

    if final_norm:
        @pl.when(f == pl.num_programs(1) - 1)
        def _():
            o_ref[...] = _rms(o_ref[...], gf_ref[...])

    if convert_next:
        cast_w13_ref[...] = next_w13_ref[...].astype(BF16)
        cast_w2_ref[...] = next_w2_ref[...].astype(BF16)


def _ffn(x, gain, w13, w2, *, final_gain=None, next_weights=None, tm=1024, tf=512):
    t, d = x.shape
    d_ff = w2.shape[0]
    gm, nf = t // tm, d_ff // tf
    in_specs = [
        pl.BlockSpec((tm, d), lambda m, f: (m, 0)),
        pl.BlockSpec((1, d), lambda m, f: (0, 0)),
        pl.BlockSpec((d, tf), lambda m, f: (0, f)),
        pl.BlockSpec((d, tf), lambda m, f: (0, f + nf)),
        pl.BlockSpec((tf, d), lambda m, f: (f, 0)),
    ]
    args = [x, gain, w13, w13, w2]
    out_specs = [pl.BlockSpec((tm, d), lambda m, f: (m, 0))]
    out_shape = [jax.ShapeDtypeStruct((t, d), F32)]
    if final_gain is not None:
        in_specs.append(pl.BlockSpec((1, d), lambda m, f: (0, 0)))
        args.append(final_gain)
    if next_weights is not None:
        n13, n2, nl = next_weights
        b13 = (d // gm, 2 * d_ff // nf)
        b2 = (d_ff // nf, d // gm)
        in_specs += [pl.BlockSpec((None,) + b13, lambda m, f: (nl, m, f)),
                     pl.BlockSpec((None,) + b2, lambda m, f: (nl, f, m))]
        args += [n13, n2]
        out_specs += [pl.BlockSpec(b13, lambda m, f: (m, f)), pl.BlockSpec(b2, lambda m, f: (f, m))]
        out_shape += [jax.ShapeDtypeStruct(n13.shape[1:], BF16),
                      jax.ShapeDtypeStruct(n2.shape[1:], BF16)]
    outs = pl.pallas_call(
        functools.partial(_ffn_kernel, final_norm=final_gain is not None,
                          convert_next=next_weights is not None),
        grid=(gm, nf),
        in_specs=in_specs,
        out_specs=out_specs,
        out_shape=out_shape,
        scratch_shapes=[pltpu.VMEM((tm, d), BF16)],
        compiler_params=_params("parallel", "arbitrary"),
        name="ffn",
    )(*args)
    return outs if next_weights is not None else (outs[0], None, None)


def _regroup_layout(n_src, gate_start, gate_rows, n_chunks):
    body = n_src - gate_rows
    chunk = -(-(body + LANES) // (n_chunks * LANES)) * LANES
    return body, n_chunks * chunk, chunk


def _regroup_tile(main_ref, next_ref, gate_ref, out_ref, j, *, n_src, gate_start, gate_rows,
                  n_chunks):
    body, _, chunk = _regroup_layout(n_src, gate_start, gate_rows, n_chunks)
    assert body % LANES == 0 and gate_start % SUBLANES == 0 and gate_rows % SUBLANES == 0

    def shifted():
        ext = jnp.concatenate([main_ref[...], next_ref[...]], axis=0)
        return ext[gate_rows:gate_rows + chunk]

    for c in range(n_chunks):
        lo, hi = c * chunk, (c + 1) * chunk

        @pl.when(j == c)
        def _():
            pieces = []
            if lo < gate_start:
                pieces.append(main_ref[:min(hi, gate_start) - lo])
            if max(lo, gate_start) < min(hi, body):
                pieces.append(shifted()[max(lo, gate_start) - lo:min(hi, body) - lo])
            if lo <= body < hi:
                pieces.append(gate_ref[...])
            done = sum(p.shape[0] for p in pieces)
            if done < chunk:
                pieces.append(jnp.zeros((chunk - done, out_ref.shape[1]), F32))
            out_ref[...] = jnp.concatenate(pieces, axis=0).astype(out_ref.dtype)


def _in_proj_kernel(blocks_ref, *refs, n_f32, regroup):
    del blocks_ref
    refs = list(refs)
    x_ref, g_ref, w_ref, wg_ref = refs[:4]
    del refs[:4]
    main_ref, next_ref, gate_ref = (refs.pop(0), refs.pop(0), refs.pop(0)) if regroup else (None,) * 3
    za_ref, zb_ref, fl_ref = refs[:3]
    del refs[:3]
    cast_ref = refs.pop(0) if regroup else None
    (xn_ref,) = refs
    j = pl.program_id(1)

    @pl.when(j == 0)
    def _():
        xn_ref[...] = _rms(x_ref[...], g_ref[...]).astype(BF16)
        fl_ref[...] = _dot_nt(xn_ref[...], wg_ref[...])

    @pl.when(j < n_f32)
    def _():
        za_ref[...] = _dot_nt(xn_ref[...], w_ref[...])

    @pl.when(j >= n_f32)
    def _():
        zb_ref[...] = _dot_nt(xn_ref[...], w_ref[...]).astype(zb_ref.dtype)

    if regroup:
        _regroup_tile(main_ref, next_ref, gate_ref, cast_ref, j, **regroup)


def _in_proj(x, gain, w, blocks_f32, blocks_bf16, gate_block, *, tn, regroup_next=None, tm=1024):
    t, d = x.shape
    n_f32, n_bf16 = len(blocks_f32), len(blocks_bf16)
    gm, gn = t // tm, n_f32 + n_bf16
    table = jnp.asarray(list(blocks_f32) + list(blocks_bf16), jnp.int32)
    in_specs = [
        pl.BlockSpec((tm, d), lambda m, j, tab: (m, 0)),
        pl.BlockSpec((1, d), lambda m, j, tab: (0, 0)),
        pl.BlockSpec((tn, d), lambda m, j, tab: (tab[j], 0)),
        pl.BlockSpec((LANES, d), lambda m, j, tab: (gate_block, 0)),
    ]
    args = [x, gain, w, w]
    out_specs = [
        pl.BlockSpec((tm, tn), lambda m, j, tab: (m, jnp.minimum(j, n_f32 - 1))),
        pl.BlockSpec((tm, tn), lambda m, j, tab: (m, jnp.maximum(j - n_f32, 0))),
        pl.BlockSpec((tm, LANES), lambda m, j, tab: (m, 0)),
    ]
    out_shape = [jax.ShapeDtypeStruct((t, n_f32 * tn), F32),
                 jax.ShapeDtypeStruct((t, n_bf16 * tn), BF16),
                 jax.ShapeDtypeStruct((t, LANES), F32)]
    regroup = None
    if regroup_next is not None:
        src, nl, gate_start, gate_rows = regroup_next
        n_src = src.shape[1]
        regroup = dict(n_src=n_src, gate_start=gate_start, gate_rows=gate_rows, n_chunks=gn)
        _, width, chunk = _regroup_layout(**regroup)
        cols = d // gm
        in_specs += [
            pl.BlockSpec((None, chunk, cols),
                         lambda m, j, tab: (nl, jnp.minimum(j, (n_src - 1) // chunk), m)),
            pl.BlockSpec((None, gate_rows, cols),
                         lambda m, j, tab: (nl, jnp.minimum((j + 1) * (chunk // gate_rows),
                                                            (n_src - 1) // gate_rows), m)),
            pl.BlockSpec((None, gate_rows, cols), lambda m, j, tab: (nl, gate_start // gate_rows, m)),
        ]
        args += [src, src, src]
        out_specs.append(pl.BlockSpec((chunk, cols), lambda m, j, tab: (j, m)))
        out_shape.append(jax.ShapeDtypeStruct((width, d), BF16))
    outs = pl.pallas_call(
        functools.partial(_in_proj_kernel, n_f32=n_f32, regroup=regroup),
        grid_spec=pltpu.PrefetchScalarGridSpec(
            num_scalar_prefetch=1, grid=(gm, gn), in_specs=in_specs, out_specs=out_specs,
            scratch_shapes=[pltpu.VMEM((tm, d), BF16)]),
        out_shape=out_shape,
        compiler_params=_params("parallel", "arbitrary"),
        name="in_proj",
    )(table, *args)
    return tuple(outs) if regroup_next is not None else tuple(outs) + (None,)


def _forget_cumsum_kernel(fl_ref, b_ref, cf_ref, cft_ref, *, heads, tk):
    z = fl_ref[...] + b_ref[...]
    x = jnp.minimum(z, 0.0) - jnp.log1p(jnp.exp(-jnp.abs(z)))
    s = x.shape[0]
    row = lax.broadcasted_iota(jnp.int32, x.shape, 0)
    shift = 1
    while shift < s:
        if shift < SUBLANES:
            prev = jnp.where(row >= shift, pltpu.roll(x, shift, axis=0), 0.0)
        else:
            prev = jnp.concatenate([jnp.zeros((shift, x.shape[1]), F32), x[:s - shift]], axis=0)
        x = x + prev
        shift *= 2
    cf_ref[...] = x
    xt = x.T
    for j in range(s // tk):
        cft_ref[j] = xt[:heads, j * tk:(j + 1) * tk]


def _forget_cumsum(fl, bias, layer, *, heads, tk):
    b, s, w = fl.shape
    return pl.pallas_call(
        functools.partial(_forget_cumsum_kernel, heads=heads, tk=tk),
        grid=(b,),
        in_specs=[
            pl.BlockSpec((None, s, w), lambda i: (i, 0, 0)),
            pl.BlockSpec((None, 1, w), lambda i: (layer, 0, 0)),
        ],
        out_specs=[
            pl.BlockSpec((None, s, w), lambda i: (i, 0, 0)),
            pl.BlockSpec((None, s // tk, heads, tk), lambda i: (i, 0, 0, 0)),
        ],
        out_shape=[
            jax.ShapeDtypeStruct((b, s, w), F32),
            jax.ShapeDtypeStruct((b, s // tk, heads, tk), F32),
        ],
        compiler_params=_params("parallel"),
        name="forget_cumsum",
    )(fl, bias)


def _fox_kernel(q_ref, k_ref, v_ref, cf_ref, cft_ref, o_ref, fq_ref, m_ref, accl_ref, *, tq):
    i = pl.program_id(1)
    heads = q_ref.shape[-1] // HEAD_DIM
    log2e = 1.4426950408889634
    c1 = HEAD_DIM ** -0.5 * log2e
    causal = (lax.broadcasted_iota(jnp.int32, (tq, tq), 0)
              >= lax.broadcasted_iota(jnp.int32, (tq, tq), 1))
    ones = jnp.ones((tq, HEAD_DIM), BF16)
    twice = lambda a: jnp.concatenate([a, a], axis=1)

    cf = cf_ref[...] * log2e
    for h in range(heads):
        fq_ref[h] = jnp.broadcast_to(cf[:, h:h + 1], (tq, LANES))
    m_ref[...] = jnp.full(m_ref.shape, -jnp.inf, F32)
    accl_ref[...] = jnp.zeros(accl_ref.shape, F32)

    def block(j, masked):
        k0 = pl.multiple_of(j * tq, tq)
        for h in range(heads):
            lanes = slice(h * HEAD_DIM, (h + 1) * HEAD_DIM)
            kb = k_ref[pl.ds(k0, tq), lanes]
            v1 = jnp.concatenate([v_ref[pl.ds(k0, tq), lanes], ones], axis=1)
            fk = cft_ref[j, h:h + 1, :] * log2e
            t = _dot_nt(q_ref[:, lanes], kb) * c1 + twice(fq_ref[h]) - fk
            if masked:
                t = jnp.where(causal, t, -jnp.inf)
            m_prev = m_ref[h]
            m_new = jnp.maximum(m_prev, jnp.max(t, axis=-1, keepdims=True))
            p = jnp.exp2(t - twice(m_new))
            accl_ref[h] = twice(jnp.exp2(m_prev - m_new)) * accl_ref[h] + _dot(p.astype(BF16), v1)
            m_ref[h] = m_new

    lax.fori_loop(0, i, lambda j, c: (block(j, False), c)[1], 0)
    block(i, True)
    for h in range(heads):
        accl = accl_ref[h]
        o_ref[:, h * HEAD_DIM:(h + 1) * HEAD_DIM] = (
            accl[:, :HEAD_DIM] / accl[:, HEAD_DIM:]).astype(o_ref.dtype)


def _fox(zb, cf, cft, *, width, tq):
    b, s, _ = zb.shape
    heads = width // HEAD_DIM
    return pl.pallas_call(
        functools.partial(_fox_kernel, tq=tq),
        grid=(b, s // tq),
        in_specs=[
            pl.BlockSpec((None, tq, width), lambda bi, i: (bi, i, 0)),
            pl.BlockSpec((None, s, width), lambda bi, i: (bi, 0, 1)),
            pl.BlockSpec((None, s, width), lambda bi, i: (bi, 0, 2)),
            pl.BlockSpec((None, tq, LANES), lambda bi, i: (bi, i, 0)),
            pl.BlockSpec((None, s // tq, heads, tq), lambda bi, i: (bi, 0, 0, 0)),
        ],
        out_specs=pl.BlockSpec((None, tq, width), lambda bi, i: (bi, i, 0)),
        out_shape=jax.ShapeDtypeStruct((b, s, width), BF16),
        scratch_shapes=[pltpu.VMEM((heads, tq, LANES), F32),
                        pltpu.VMEM((heads, tq, LANES), F32),
                        pltpu.VMEM((heads, tq, 2 * HEAD_DIM), F32)],
        compiler_params=_params("parallel", "arbitrary"),
        name="fox",
    )(zb, zb, zb, cf, cft)


def _ret_kernel(q_ref, k_ref, v_ref, gate_ref, cos_ref, sin_ref, dm_ref, xi_ref, zeta_ref,
                cd_ref, o_ref, state_ref, *, chunks):
    heads = state_ref.shape[0]

    @pl.when(pl.program_id(1) == 0)
    def _():
        state_ref[...] = jnp.zeros_like(state_ref)

    for c in range(chunks):
        r = slice(c * CHUNK, (c + 1) * CHUNK)
        cos, sin = cos_ref[r, :], sin_ref[r, :]
        for h in range(heads):
            lanes = slice(h * HEAD_DIM, (h + 1) * HEAD_DIM)
            q, k = q_ref[r, lanes], k_ref[r, lanes]
            q = q * cos + pltpu.roll(q, HEAD_DIM // 2, axis=1) * sin
            k = (k * cos + pltpu.roll(k, HEAD_DIM // 2, axis=1) * sin) * (HEAD_DIM ** -0.5)
            v = v_ref[r, lanes]
            state = state_ref[h]
            scores = _dot_nt(q.astype(BF16), k.astype(BF16)) * dm_ref[h]
            o = _dot(scores.astype(BF16), v)
            o = o + _dot((q * xi_ref[h]).astype(BF16), state.astype(BF16))
            kz_t = (k * zeta_ref[h]).T.astype(BF16)
            state_ref[h] = cd_ref[h, 0:1, :] * state + _dot(kz_t, v)
            mu = jnp.mean(o, axis=-1, keepdims=True)
            var = jnp.mean(jnp.square(o - mu), axis=-1, keepdims=True)
            o = (o - mu) * lax.rsqrt(var + GN_EPS)
            o_ref[r, lanes] = (_silu(gate_ref[r, lanes]) * o).astype(o_ref.dtype)


def _retention(za, zb, tables, *, q_blk, k_blk, gate_blk, v_blk, heads, tr):
    b, s, _ = za.shape
    width = heads * HEAD_DIM
    cos, sin, dmat, xi, zeta, cdec = tables
    tok = lambda off: pl.BlockSpec((None, tr, width), lambda bi, i: (bi, i, off))
    whole = lambda a: pl.BlockSpec(a.shape, lambda bi, i: (0, 0, 0))
    pos = pl.BlockSpec((tr, HEAD_DIM), lambda bi, i: (i, 0))
    return pl.pallas_call(
        functools.partial(_ret_kernel, chunks=tr // CHUNK),
        grid=(b, s // tr),
        in_specs=[tok(q_blk), tok(k_blk), tok(v_blk), tok(gate_blk), pos, pos,
                  whole(dmat), whole(xi), whole(zeta), whole(cdec)],
        out_specs=pl.BlockSpec((None, tr, width), lambda bi, i: (bi, i, 0)),
        out_shape=jax.ShapeDtypeStruct((b, s, width), BF16),
        scratch_shapes=[pltpu.VMEM((heads, HEAD_DIM, HEAD_DIM), F32)],
        compiler_params=_params("parallel", "arbitrary"),
        name="retention",
    )(za, za, zb, za, cos, sin, dmat, xi, zeta, cdec)


def _retention_tables(s, heads):
    half = HEAD_DIM // 2
    inv_freq = ROPE_BASE ** (-jnp.arange(half, dtype=F32) / half)
    ang = jnp.arange(s, dtype=F32)[:, None] * inv_freq[None, :]
    cos = jnp.concatenate([jnp.cos(ang), jnp.cos(ang)], axis=-1)
    sin = jnp.concatenate([-jnp.sin(ang), jnp.sin(ang)], axis=-1)
    log_gamma = jnp.log1p(-jnp.power(2.0, -5.0 - jnp.arange(heads, dtype=F32)))
    pos = jnp.arange(CHUNK, dtype=F32)
    diff = pos[:, None] - pos[None, :]
    dmat = jnp.where(diff[None] >= 0,
                     jnp.exp(jnp.maximum(diff, 0.0)[None] * log_gamma[:, None, None]), 0.0)
    xi = jnp.exp((pos + 1.0)[None, :] * log_gamma[:, None])
    zeta = jnp.exp((CHUNK - 1.0 - pos)[None, :] * log_gamma[:, None])
    cdec = jnp.exp(CHUNK * log_gamma)
    wide = lambda a: jnp.broadcast_to(a[:, :, None], (heads, CHUNK, HEAD_DIM))
    return (cos, sin, dmat, wide(xi), wide(zeta),
            jnp.broadcast_to(cdec[:, None, None], (heads, SUBLANES, HEAD_DIM)))


def _pool_kernel(u_ref, w_ref, sc_ref, o_ref, ext_ref, *, ts):
    s = pl.program_id(1)
    gd = u_ref.shape[-1] // len(POOL_WINDOWS)

    @pl.when(s == 0)
    def _():
        ext_ref[0:POOL_HALO, :] = jnp.zeros((POOL_HALO, ext_ref.shape[1]), F32)

    @pl.when(s > 0)
    def _():
        ext_ref[0:POOL_HALO, :] = ext_ref[ts:ts + POOL_HALO, :]

    ext_ref[POOL_HALO:POOL_HALO + ts, :] = u_ref[...]
    t = s * ts + lax.broadcasted_iota(jnp.int32, (ts, 1), 0)
    for g, w in enumerate(POOL_WINDOWS):
        cols = slice(g * gd, (g + 1) * gd)
        u = ext_ref[POOL_HALO:POOL_HALO + ts, cols]
        win = u
        for k in range(1, w):
            win = win + ext_ref[POOL_HALO - k:POOL_HALO - k + ts, cols]
        cnt = jnp.minimum(t + 1, w).astype(F32)
        pooled = win / cnt - u
        mixed = _dot(pooled.astype(BF16), w_ref[g])
        o_ref[:, cols] = (mixed * sc_ref[:, cols]).astype(o_ref.dtype)


def _pool(za, pool_w, pool_scale, layer, *, width, ts):
    b, s, _ = za.shape
    groups, gd = pool_w.shape[1], pool_w.shape[2]
    return pl.pallas_call(
        functools.partial(_pool_kernel, ts=ts),
        grid=(b, s // ts),
        in_specs=[
            pl.BlockSpec((None, ts, width), lambda bi, i: (bi, i, 0)),
            pl.BlockSpec((None, groups, gd, gd), lambda bi, i: (layer, 0, 0, 0)),
            pl.BlockSpec((None, 1, width), lambda bi, i: (layer, 0, 0)),
        ],
        out_specs=pl.BlockSpec((None, ts, width), lambda bi, i: (bi, i, 0)),
        out_shape=jax.ShapeDtypeStruct((b, s, width), BF16),
        scratch_shapes=[pltpu.VMEM((POOL_HALO + ts, width), F32)],
        compiler_params=_params("parallel", "arbitrary"),
        name="pool",
    )(za, pool_w, pool_scale)


def _merge_kernel(*refs, convert_next):
    yp_ref, yf_ref, yr_ref, g0_ref, g1_ref, g2_ref, x_ref, wp_ref, wf_ref, wr_ref, wo_ref = refs[:11]
    n = 4 if convert_next else 0
    next_refs, o_ref, cast_refs = refs[11:11 + n], refs[11 + n], refs[12 + n:]
    merged = (jax.nn.sigmoid(g0_ref[...]) * _dot(yp_ref[...], wp_ref[...])
              + jax.nn.sigmoid(g1_ref[...]) * _dot(yf_ref[...], wf_ref[...])
              + jax.nn.sigmoid(g2_ref[...]) * _dot(yr_ref[...], wr_ref[...]))
    o_ref[...] = x_ref[...] + _dot(merged.astype(BF16), wo_ref[...])
    for src_ref, dst_ref in zip(next_refs, cast_refs):
        dst_ref[...] = src_ref[...].astype(BF16)


def _merge(y_pool, y_fox, y_ret, za, x, weights, *, gate_blk, next_weights=None, tm=256):
    t, d = x.shape
    gm = t // tm
    branch = lambda y: pl.BlockSpec((tm, y.shape[-1]), lambda m: (m, 0))
    gate = lambda i: pl.BlockSpec((tm, d), lambda m: (m, gate_blk + i))
    resident = lambda w: pl.BlockSpec(w.shape, lambda m: (0, 0), pipeline_mode=pl.Buffered(1))
    in_specs = [branch(y_pool), branch(y_fox), branch(y_ret), gate(0), gate(1), gate(2),
                pl.BlockSpec((tm, d), lambda m: (m, 0))] + [resident(w) for w in weights]
    args = [y_pool, y_fox, y_ret, za, za, za, x, *weights]
    out_specs = [pl.BlockSpec((tm, d), lambda m: (m, 0))]
    out_shape = [jax.ShapeDtypeStruct((t, d), F32)]
    if next_weights is not None:
        *stacked, nl = next_weights
        for w in stacked:
            rows = w.shape[1] // gm
            in_specs.append(pl.BlockSpec((None, rows, w.shape[2]), lambda m: (nl, m, 0)))
            out_specs.append(pl.BlockSpec((rows, w.shape[2]), lambda m: (m, 0)))
            out_shape.append(jax.ShapeDtypeStruct(w.shape[1:], BF16))
        args += stacked
    outs = pl.pallas_call(
        functools.partial(_merge_kernel, convert_next=next_weights is not None),
        grid=(gm,),
        in_specs=in_specs,
        out_specs=out_specs,
        out_shape=out_shape,
        compiler_params=_params("parallel"),
        name="merge",
    )(*args)
    return outs[0], (tuple(outs[1:]) if next_weights is not None else None)


def kernel(x, ffn1_norm, ffn1_w13, ffn1_w2, mix_norm, w_in, forget_bias, pool_w, pool_scale,
           w_branch_pool, w_branch_fox, w_branch_ret, w_out, ffn2_norm, ffn2_w13, ffn2_w2,
           final_norm):
    b, s, d = x.shape
    depth = w_in.shape[0]
    pool_width = pool_scale.shape[-1]
    fox_heads = forget_bias.shape[-1]
    fox_width = w_branch_fox.shape[1]
    ret_width = w_branch_ret.shape[1]
    ret_heads = ret_width // HEAD_DIM
    assert fox_width == fox_heads * HEAD_DIM and pool_width == fox_width == ret_width
    assert d % fox_width == 0

    o = 0
    cuts = {}
    for name, width in (("pool", pool_width), ("fq", fox_width), ("fk", fox_width),
                        ("fv", fox_width), ("fl", fox_heads), ("rq", ret_width),
                        ("rk", ret_width), ("rv", ret_width), ("rg", ret_width),
                        ("gates", N_BRANCHES * d)):
        cuts[name] = slice(o, o + width)
        o += width
    assert o == w_in.shape[-1]
    w_in_t = jnp.swapaxes(w_in, 1, 2)
    gate_start, gate_rows = cuts["fl"].start, fox_heads
    tn = pool_width
    moved = lambda name: cuts[name].start - (gate_rows if cuts[name].start > gate_start else 0)
    blocks_of = lambda name: [moved(name) // tn + i for i in range((cuts[name].stop - cuts[name].start) // tn)]
    assert all(moved(n) % tn == 0 for n in cuts if n != "fl")
    blocks_a = sum((blocks_of(n) for n in ("pool", "rq", "rk", "rg", "gates")), [])
    blocks_b = sum((blocks_of(n) for n in ("fq", "fk", "fv", "rv")), [])
    body, width, _ = _regroup_layout(w_in.shape[-1], gate_start, gate_rows,
                                     len(blocks_a) + len(blocks_b))
    src0 = w_in_t[0]
    w_r = jnp.concatenate([src0[:gate_start], src0[gate_start + gate_rows:],
                           src0[gate_start:gate_start + gate_rows],
                           jnp.zeros((width - w_in.shape[-1], d), F32)], axis=0).astype(BF16)
    rq_blk, rk_blk, rg_blk = 1, 2, 3
    gate_blk = (pool_width + 3 * ret_width) // d
    rv_blk = 3

    bias = jnp.pad(forget_bias, ((0, 0), (0, LANES - fox_heads)))[:, None, :]
    row = lambda a: a[:, None, :]
    pool_w_b = pool_w.astype(BF16)
    branch_f32 = (w_branch_pool, w_branch_fox, w_branch_ret, w_out)
    branch_w = tuple(w[0].astype(BF16) for w in branch_f32)
    tables = _retention_tables(s, ret_heads)
    tq = 256

    halves = [(norm, w13, w2, l) for l in range(depth)
              for norm, w13, w2 in ((ffn1_norm, ffn1_w13, ffn1_w2), (ffn2_norm, ffn2_w13, ffn2_w2))]

    def half_step(xt, i, w13_b, w2_b):
        norm, _, _, l = halves[i]
        last = i == len(halves) - 1
        nxt = None if last else (halves[i + 1][1], halves[i + 1][2], halves[i + 1][3])
        return _ffn(xt, norm[l][None, :], w13_b, w2_b, next_weights=nxt,
                    final_gain=final_norm[None, :] if last else None)

    xt = x.reshape(b * s, d)
    w13_b, w2_b = ffn1_w13[0].astype(BF16), ffn1_w2[0].astype(BF16)
    for l in range(depth):
        xt, w13_b, w2_b = half_step(xt, 2 * l, w13_b, w2_b)
        gain = mix_norm[l][None, :]
        nxt = (w_in_t, l + 1, gate_start, gate_rows) if l + 1 < depth else None
        za, zb, fl, w_r_next = _in_proj(xt, gain, w_r, blocks_a, blocks_b, body // LANES, tn=tn,
                                        regroup_next=nxt)
        za3, zb3 = za.reshape(b, s, -1), zb.reshape(b, s, -1)
        cf, cft = _forget_cumsum(fl.reshape(b, s, LANES), bias, l, heads=fox_heads, tk=tq)
        y_fox = _fox(zb3, cf, cft, width=fox_width, tq=tq)
        y_ret = _retention(za3, zb3, tables, q_blk=rq_blk, k_blk=rk_blk, gate_blk=rg_blk,
                           v_blk=rv_blk, heads=ret_heads, tr=4 * CHUNK)
        y_pool = _pool(za3, pool_w_b, row(pool_scale), l, width=pool_width, ts=512)
        flat = lambda y: y.reshape(b * s, -1)
        xt, branch_w = _merge(flat(y_pool), flat(y_fox), flat(y_ret), za, xt, branch_w,
                              gate_blk=gate_blk,
                              next_weights=(*branch_f32, l + 1) if l + 1 < depth else None)
        xt, w13_b, w2_b = half_step(xt, 2 * l + 1, w13_b, w2_b)
        w_r = w_r_next
    return xt.reshape(b, s, d)
```

```python
import functools

import jax
import jax.numpy as jnp
from jax import lax
from jax.experimental import pallas as pl
from jax.experimental.pallas import tpu as pltpu

F32 = jnp.float32
BF16 = jnp.bfloat16

RMS_EPS = 1e-6
GN_EPS = 1e-5
ROPE_BASE = 10000.0
POOL_WINDOWS = (2, 4, 8, 16)
POOL_HALO = 16
HEAD_DIM = 128
CHUNK = 128
N_BRANCHES = 3

LANES = 128
SUBLANES = 8
VMEM_LIMIT_BYTES = 60 * 1024 * 1024


def _params(*semantics):
    return pltpu.CompilerParams(dimension_semantics=semantics,
                                vmem_limit_bytes=VMEM_LIMIT_BYTES)


def _rms(x, g):
    return x * lax.rsqrt(jnp.mean(x * x, axis=-1, keepdims=True) + RMS_EPS) * g


def _silu(x):
    return x * jax.nn.sigmoid(x)


def _dot(a, b):
    return jnp.dot(a, b, preferred_element_type=F32)


def _dot_nt(a, b):
    return lax.dot_general(a, b, (((1,), (1,)), ((), ())), preferred_element_type=F32)


def _ffn_kernel(*refs, final_norm, convert_next):
    refs = list(refs)
    x_ref, g_ref, w1_ref, w3_ref, w2_ref = refs[:5]
    del refs[:5]
    gf_ref = refs.pop(0) if final_norm else None
    next_w13_ref, next_w2_ref = (refs.pop(0), refs.pop(0)) if convert_next else (None, None)
    o_ref = refs.pop(0)
    cast_w13_ref, cast_w2_ref = (refs.pop(0), refs.pop(0)) if convert_next else (None, None)
    (xn_ref,) = refs
    f = pl.program_id(1)

    @pl.when(f == 0)
    def _():
        x = x_ref[...]
        xn_ref[...] = _rms(x, g_ref[...]).astype(BF16)
        o_ref[...] = x

    xn = xn_ref[...]
    h = (_silu(_dot(xn, w1_ref[...])) * _dot(xn, w3_ref[...])).astype(BF16)
    o_ref[...] += 0.5 * _dot(h, w2_ref[...])

    if final_norm:
        @pl.when(f == pl.num_programs(1) - 1)
        def _():
            o_ref[...] = _rms(o_ref[...], gf_ref[...])

    if convert_next:
        cast_w13_ref[...] = next_w13_ref[...].astype(BF16)
        cast_w2_ref[...] = next_w2_ref[...].astype(BF16)


def _ffn(x, gain, w13, w2, *, final_gain=None, next_weights=None, tm=1024, tf=512):
    t, d = x.shape
    d_ff = w2.shape[0]
    gm, nf = t // tm, d_ff // tf
    in_specs = [
        pl.BlockSpec((tm, d), lambda m, f: (m, 0)),
        pl.BlockSpec((1, d), lambda m, f: (0, 0)),
        pl.BlockSpec((d, tf), lambda m, f: (0, f)),
        pl.BlockSpec((d, tf), lambda m, f: (0, f + nf)),
        pl.BlockSpec((tf, d), lambda m, f: (f, 0)),
    ]
    args = [x, gain, w13, w13, w2]
    out_specs = [pl.BlockSpec((tm, d), lambda m, f: (m, 0))]
    out_shape = [jax.ShapeDtypeStruct((t, d), F32)]
    if final_gain is not None:
        in_specs.append(pl.BlockSpec((1, d), lambda m, f: (0, 0)))
        args.append(final_gain)
    if next_weights is not None:
        n13, n2, nl = next_weights
        b13 = (d // gm, 2 * d_ff // nf)
        b2 = (d_ff // nf, d // gm)
        in_specs += [pl.BlockSpec((None,) + b13, lambda m, f: (nl, m, f)),
                     pl.BlockSpec((None,) + b2, lambda m, f: (nl, f, m))]
        args += [n13, n2]
        out_specs += [pl.BlockSpec(b13, lambda m, f: (m, f)), pl.BlockSpec(b2, lambda m, f: (f, m))]
        out_shape += [jax.ShapeDtypeStruct(n13.shape[1:], BF16),
                      jax.ShapeDtypeStruct(n2.shape[1:], BF16)]
    outs = pl.pallas_call(
        functools.partial(_ffn_kernel, final_norm=final_gain is not None,
                          convert_next=next_weights is not None),
        grid=(gm, nf),
        in_specs=in_specs,
        out_specs=out_specs,
        out_shape=out_shape,
        scratch_shapes=[pltpu.VMEM((tm, d), BF16)],
        compiler_params=_params("parallel", "arbitrary"),
        name="ffn",
    )(*args)
    return outs if next_weights is not None else (outs[0], None, None)


def _in_proj_kernel(rows_ref, x_ref, g_ref, w_ref, wg_ref, za_ref, zb_ref, fl_ref, xn_ref, *,
                    n_f32, gate_rows):
    del rows_ref
    j = pl.program_id(1)

    @pl.when(j == 0)
    def _():
        xn_ref[...] = _rms(x_ref[...], g_ref[...]).astype(BF16)
        row = lax.broadcasted_iota(jnp.int32, wg_ref.shape[1:], 0)
        wg = jnp.where(row < gate_rows, wg_ref[0], 0.0).astype(BF16)
        fl_ref[...] = _dot_nt(xn_ref[...], wg)

    @pl.when(j < n_f32)
    def _():
        za_ref[...] = _dot_nt(xn_ref[...], w_ref[0].astype(BF16))

    @pl.when(j >= n_f32)
    def _():
        zb_ref[...] = _dot_nt(xn_ref[...], w_ref[0].astype(BF16)).astype(zb_ref.dtype)


def _in_proj(x, gain, w_t, layer, rows_f32, rows_bf16, gate_start, gate_rows, *, tn, tm=1024):
    t, d = x.shape
    n_f32, n_bf16 = len(rows_f32), len(rows_bf16)
    gm, gn = t // tm, n_f32 + n_bf16
    assert all(r % SUBLANES == 0 for r in list(rows_f32) + list(rows_bf16) + [gate_start])
    table = jnp.asarray(list(rows_f32) + list(rows_bf16), jnp.int32) // SUBLANES
    return pl.pallas_call(
        functools.partial(_in_proj_kernel, n_f32=n_f32, gate_rows=gate_rows),
        grid_spec=pltpu.PrefetchScalarGridSpec(
            num_scalar_prefetch=1, grid=(gm, gn),
            in_specs=[
                pl.BlockSpec((tm, d), lambda m, j, tab: (m, 0)),
                pl.BlockSpec((1, d), lambda m, j, tab: (0, 0)),
                pl.BlockSpec((pl.Element(1), pl.Element(tn), pl.Element(d)),
                             lambda m, j, tab: (layer, tab[j] * SUBLANES, 0)),
                pl.BlockSpec((pl.Element(1), pl.Element(LANES), pl.Element(d)),
                             lambda m, j, tab: (layer, gate_start, 0)),
            ],
            out_specs=[
                pl.BlockSpec((tm, tn), lambda m, j, tab: (m, jnp.minimum(j, n_f32 - 1))),
                pl.BlockSpec((tm, tn), lambda m, j, tab: (m, jnp.maximum(j - n_f32, 0))),
                pl.BlockSpec((tm, LANES), lambda m, j, tab: (m, 0)),
            ],
            scratch_shapes=[pltpu.VMEM((tm, d), BF16)]),
        out_shape=[jax.ShapeDtypeStruct((t, n_f32 * tn), F32),
                   jax.ShapeDtypeStruct((t, n_bf16 * tn), BF16),
                   jax.ShapeDtypeStruct((t, LANES), F32)],
        compiler_params=_params("parallel", "arbitrary"),
        name="in_proj",
    )(table, x, gain, w_t, w_t)


def _forget_cumsum_kernel(fl_ref, b_ref, cf_ref, cft_ref, *, heads, tk):
    z = fl_ref[...] + b_ref[...]
    x = jnp.minimum(z, 0.0) - jnp.log1p(jnp.exp(-jnp.abs(z)))
    s = x.shape[0]
    row = lax.broadcasted_iota(jnp.int32, x.shape, 0)
    shift = 1
    while shift < s:
        if shift < SUBLANES:
            prev = jnp.where(row >= shift, pltpu.roll(x, shift, axis=0), 0.0)
        else:
            prev = jnp.concatenate([jnp.zeros((shift, x.shape[1]), F32), x[:s - shift]], axis=0)
        x = x + prev
        shift *= 2
    cf_ref[...] = x
    xt = x.T
    for j in range(s // tk):
        cft_ref[j] = xt[:heads, j * tk:(j + 1) * tk]


def _forget_cumsum(fl, bias, layer, *, heads, tk):
    b, s, w = fl.shape
    return pl.pallas_call(
        functools.partial(_forget_cumsum_kernel, heads=heads, tk=tk),
        grid=(b,),
        in_specs=[
            pl.BlockSpec((None, s, w), lambda i: (i, 0, 0)),
            pl.BlockSpec((None, 1, w), lambda i: (layer, 0, 0)),
        ],
        out_specs=[
            pl.BlockSpec((None, s, w), lambda i: (i, 0, 0)),
            pl.BlockSpec((None, s // tk, heads, tk), lambda i: (i, 0, 0, 0)),
        ],
        out_shape=[
            jax.ShapeDtypeStruct((b, s, w), F32),
            jax.ShapeDtypeStruct((b, s // tk, heads, tk), F32),
        ],
        compiler_params=_params("parallel"),
        name="forget_cumsum",
    )(fl, bias)


def _fox_kernel(q_ref, k_ref, v_ref, cf_ref, cft_ref, o_ref, fq_ref, m_ref, accl_ref, *, tq):
    i = pl.program_id(1)
    heads = q_ref.shape[-1] // HEAD_DIM
    log2e = 1.4426950408889634
    c1 = HEAD_DIM ** -0.5 * log2e
    causal = (lax.broadcasted_iota(jnp.int32, (tq, tq), 0)
              >= lax.broadcasted_iota(jnp.int32, (tq, tq), 1))
    ones = jnp.ones((tq, HEAD_DIM), BF16)
    twice = lambda a: jnp.concatenate([a, a], axis=1)

    cf = cf_ref[...] * log2e
    for h in range(heads):
        fq_ref[h] = jnp.broadcast_to(cf[:, h:h + 1], (tq, LANES))
    m_ref[...] = jnp.full(m_ref.shape, -jnp.inf, F32)
    accl_ref[...] = jnp.zeros(accl_ref.shape, F32)

    def block(j, masked):
        k0 = pl.multiple_of(j * tq, tq)
        for h in range(heads):
            lanes = slice(h * HEAD_DIM, (h + 1) * HEAD_DIM)
            kb = k_ref[pl.ds(k0, tq), lanes]
            v1 = jnp.concatenate([v_ref[pl.ds(k0, tq), lanes], ones], axis=1)
            fk = cft_ref[j, h:h + 1, :] * log2e
            t = _dot_nt(q_ref[:, lanes], kb) * c1 + twice(fq_ref[h]) - fk
            if masked:
                t = jnp.where(causal, t, -jnp.inf)
            m_prev = m_ref[h]
            m_new = jnp.maximum(m_prev, jnp.max(t, axis=-1, keepdims=True))
            p = jnp.exp2(t - twice(m_new))
            accl_ref[h] = twice(jnp.exp2(m_prev - m_new)) * accl_ref[h] + _dot(p.astype(BF16), v1)
            m_ref[h] = m_new

    lax.fori_loop(0, i, lambda j, c: (block(j, False), c)[1], 0)
    block(i, True)
    for h in range(heads):
        accl = accl_ref[h]
        o_ref[:, h * HEAD_DIM:(h + 1) * HEAD_DIM] = (
            accl[:, :HEAD_DIM] / accl[:, HEAD_DIM:]).astype(o_ref.dtype)


def _fox(zb, cf, cft, *, width, tq):
    b, s, _ = zb.shape
    heads = width // HEAD_DIM
    return pl.pallas_call(
        functools.partial(_fox_kernel, tq=tq),
        grid=(b, s // tq),
        in_specs=[
            pl.BlockSpec((None, tq, width), lambda bi, i: (bi, i, 0)),
            pl.BlockSpec((None, s, width), lambda bi, i: (bi, 0, 1)),
            pl.BlockSpec((None, s, width), lambda bi, i: (bi, 0, 2)),
            pl.BlockSpec((None, tq, LANES), lambda bi, i: (bi, i, 0)),
            pl.BlockSpec((None, s // tq, heads, tq), lambda bi, i: (bi, 0, 0, 0)),
        ],
        out_specs=pl.BlockSpec((None, tq, width), lambda bi, i: (bi, i, 0)),
        out_shape=jax.ShapeDtypeStruct((b, s, width), BF16),
        scratch_shapes=[pltpu.VMEM((heads, tq, LANES), F32),
                        pltpu.VMEM((heads, tq, LANES), F32),
                        pltpu.VMEM((heads, tq, 2 * HEAD_DIM), F32)],
        compiler_params=_params("parallel", "arbitrary"),
        name="fox",
    )(zb, zb, zb, cf, cft)


def _ret_kernel(q_ref, k_ref, v_ref, gate_ref, cos_ref, sin_ref, dm_ref, xi_ref, zeta_ref,
                cd_ref, o_ref, state_ref, *, chunks):
    heads = state_ref.shape[0]

    @pl.when(pl.program_id(1) == 0)
    def _():
        state_ref[...] = jnp.zeros_like(state_ref)

    for c in range(chunks):
        r = slice(c * CHUNK, (c + 1) * CHUNK)
        cos, sin = cos_ref[r, :], sin_ref[r, :]
        for h in range(heads):
            lanes = slice(h * HEAD_DIM, (h + 1) * HEAD_DIM)
            q, k = q_ref[r, lanes], k_ref[r, lanes]
            q = q * cos + pltpu.roll(q, HEAD_DIM // 2, axis=1) * sin
            k = (k * cos + pltpu.roll(k, HEAD_DIM // 2, axis=1) * sin) * (HEAD_DIM ** -0.5)
            v = v_ref[r, lanes]
            state = state_ref[h]
            scores = _dot_nt(q.astype(BF16), k.astype(BF16)) * dm_ref[h]
            o = _dot(scores.astype(BF16), v)
            o = o + _dot((q * xi_ref[h]).astype(BF16), state.astype(BF16))
            kz_t = (k * zeta_ref[h]).T.astype(BF16)
            state_ref[h] = cd_ref[h, 0:1, :] * state + _dot(kz_t, v)
            mu = jnp.mean(o, axis=-1, keepdims=True)
            var = jnp.mean(jnp.square(o - mu), axis=-1, keepdims=True)
            o = (o - mu) * lax.rsqrt(var + GN_EPS)
            o_ref[r, lanes] = (_silu(gate_ref[r, lanes]) * o).astype(o_ref.dtype)


def _retention(za, zb, tables, *, q_blk, k_blk, gate_blk, v_blk, heads, tr):
    b, s, _ = za.shape
    width = heads * HEAD_DIM
    cos, sin, dmat, xi, zeta, cdec = tables
    tok = lambda off: pl.BlockSpec((None, tr, width), lambda bi, i: (bi, i, off))
    whole = lambda a: pl.BlockSpec(a.shape, lambda bi, i: (0, 0, 0))
    pos = pl.BlockSpec((tr, HEAD_DIM), lambda bi, i: (i, 0))
    return pl.pallas_call(
        functools.partial(_ret_kernel, chunks=tr // CHUNK),
        grid=(b, s // tr),
        in_specs=[tok(q_blk), tok(k_blk), tok(v_blk), tok(gate_blk), pos, pos,
                  whole(dmat), whole(xi), whole(zeta), whole(cdec)],
        out_specs=pl.BlockSpec((None, tr, width), lambda bi, i: (bi, i, 0)),
        out_shape=jax.ShapeDtypeStruct((b, s, width), BF16),
        scratch_shapes=[pltpu.VMEM((heads, HEAD_DIM, HEAD_DIM), F32)],
        compiler_params=_params("parallel", "arbitrary"),
        name="retention",
    )(za, za, zb, za, cos, sin, dmat, xi, zeta, cdec)


def _retention_tables(s, heads):
    half = HEAD_DIM // 2
    inv_freq = ROPE_BASE ** (-jnp.arange(half, dtype=F32) / half)
    ang = jnp.arange(s, dtype=F32)[:, None] * inv_freq[None, :]
    cos = jnp.concatenate([jnp.cos(ang), jnp.cos(ang)], axis=-1)
    sin = jnp.concatenate([-jnp.sin(ang), jnp.sin(ang)], axis=-1)
    log_gamma = jnp.log1p(-jnp.power(2.0, -5.0 - jnp.arange(heads, dtype=F32)))
    pos = jnp.arange(CHUNK, dtype=F32)
    diff = pos[:, None] - pos[None, :]
    dmat = jnp.where(diff[None] >= 0,
                     jnp.exp(jnp.maximum(diff, 0.0)[None] * log_gamma[:, None, None]), 0.0)
    xi = jnp.exp((pos + 1.0)[None, :] * log_gamma[:, None])
    zeta = jnp.exp((CHUNK - 1.0 - pos)[None, :] * log_gamma[:, None])
    cdec = jnp.exp(CHUNK * log_gamma)
    wide = lambda a: jnp.broadcast_to(a[:, :, None], (heads, CHUNK, HEAD_DIM))
    return (cos, sin, dmat, wide(xi), wide(zeta),
            jnp.broadcast_to(cdec[:, None, None], (heads, SUBLANES, HEAD_DIM)))


def _pool_kernel(u_ref, w_ref, sc_ref, o_ref, ext_ref, *, ts):
    s = pl.program_id(1)
    gd = u_ref.shape[-1] // len(POOL_WINDOWS)

    @pl.when(s == 0)
    def _():
        ext_ref[0:POOL_HALO, :] = jnp.zeros((POOL_HALO, ext_ref.shape[1]), F32)

    @pl.when(s > 0)
    def _():
        ext_ref[0:POOL_HALO, :] = ext_ref[ts:ts + POOL_HALO, :]

    ext_ref[POOL_HALO:POOL_HALO + ts, :] = u_ref[...]
    t = s * ts + lax.broadcasted_iota(jnp.int32, (ts, 1), 0)
    for g, w in enumerate(POOL_WINDOWS):
        cols = slice(g * gd, (g + 1) * gd)
        u = ext_ref[POOL_HALO:POOL_HALO + ts, cols]
        win = u
        for k in range(1, w):
            win = win + ext_ref[POOL_HALO - k:POOL_HALO - k + ts, cols]
        cnt = jnp.minimum(t + 1, w).astype(F32)
        pooled = win / cnt - u
        mixed = _dot(pooled.astype(BF16), w_ref[g])
        o_ref[:, cols] = (mixed * sc_ref[:, cols]).astype(o_ref.dtype)


def _pool(za, pool_w, pool_scale, layer, *, width, ts):
    b, s, _ = za.shape
    groups, gd = pool_w.shape[1], pool_w.shape[2]
    return pl.pallas_call(
        functools.partial(_pool_kernel, ts=ts),
        grid=(b, s // ts),
        in_specs=[
            pl.BlockSpec((None, ts, width), lambda bi, i: (bi, i, 0)),
            pl.BlockSpec((None, groups, gd, gd), lambda bi, i: (layer, 0, 0, 0)),
            pl.BlockSpec((None, 1, width), lambda bi, i: (layer, 0, 0)),
        ],
        out_specs=pl.BlockSpec((None, ts, width), lambda bi, i: (bi, i, 0)),
        out_shape=jax.ShapeDtypeStruct((b, s, width), BF16),
        scratch_shapes=[pltpu.VMEM((POOL_HALO + ts, width), F32)],
        compiler_params=_params("parallel", "arbitrary"),
        name="pool",
    )(za, pool_w, pool_scale)


def _merge_kernel(*refs, convert_next):
    yp_ref, yf_ref, yr_ref, g0_ref, g1_ref, g2_ref, x_ref, wp_ref, wf_ref, wr_ref, wo_ref = refs[:11]
    n = 4 if convert_next else 0
    next_refs, o_ref, cast_refs = refs[11:11 + n], refs[11 + n], refs[12 + n:]
    merged = (jax.nn.sigmoid(g0_ref[...]) * _dot(yp_ref[...], wp_ref[...])
              + jax.nn.sigmoid(g1_ref[...]) * _dot(yf_ref[...], wf_ref[...])
              + jax.nn.sigmoid(g2_ref[...]) * _dot(yr_ref[...], wr_ref[...]))
    o_ref[...] = x_ref[...] + _dot(merged.astype(BF16), wo_ref[...])
    for src_ref, dst_ref in zip(next_refs, cast_refs):
        dst_ref[...] = src_ref[...].astype(BF16)


def _merge(y_pool, y_fox, y_ret, za, x, weights, *, gate_blk, next_weights=None, tm=256):
    t, d = x.shape
    gm = t // tm
    branch = lambda y: pl.BlockSpec((tm, y.shape[-1]), lambda m: (m, 0))
    gate = lambda i: pl.BlockSpec((tm, d), lambda m: (m, gate_blk + i))
    resident = lambda w: pl.BlockSpec(w.shape, lambda m: (0, 0), pipeline_mode=pl.Buffered(1))
    in_specs = [branch(y_pool), branch(y_fox), branch(y_ret), gate(0), gate(1), gate(2),
                pl.BlockSpec((tm, d), lambda m: (m, 0))] + [resident(w) for w in weights]
    args = [y_pool, y_fox, y_ret, za, za, za, x, *weights]
    out_specs = [pl.BlockSpec((tm, d), lambda m: (m, 0))]
    out_shape = [jax.ShapeDtypeStruct((t, d), F32)]
    if next_weights is not None:
        *stacked, nl = next_weights
        for w in stacked:
            rows = w.shape[1] // gm
            in_specs.append(pl.BlockSpec((None, rows, w.shape[2]), lambda m: (nl, m, 0)))
            out_specs.append(pl.BlockSpec((rows, w.shape[2]), lambda m: (m, 0)))
            out_shape.append(jax.ShapeDtypeStruct(w.shape[1:], BF16))
        args += stacked
    outs = pl.pallas_call(
        functools.partial(_merge_kernel, convert_next=next_weights is not None),
        grid=(gm,),
        in_specs=in_specs,
        out_specs=out_specs,
        out_shape=out_shape,
        compiler_params=_params("parallel"),
        name="merge",
    )(*args)
    return outs[0], (tuple(outs[1:]) if next_weights is not None else None)


def kernel(x, ffn1_norm, ffn1_w13, ffn1_w2, mix_norm, w_in, forget_bias, pool_w, pool_scale,
           w_branch_pool, w_branch_fox, w_branch_ret, w_out, ffn2_norm, ffn2_w13, ffn2_w2,
           final_norm):
    b, s, d = x.shape
    depth = w_in.shape[0]
    pool_width = pool_scale.shape[-1]
    fox_heads = forget_bias.shape[-1]
    fox_width = w_branch_fox.shape[1]
    ret_width = w_branch_ret.shape[1]
    ret_heads = ret_width // HEAD_DIM
    assert fox_width == fox_heads * HEAD_DIM and pool_width == fox_width == ret_width
    assert d % fox_width == 0

    o = 0
    cuts = {}
    for name, width in (("pool", pool_width), ("fq", fox_width), ("fk", fox_width),
                        ("fv", fox_width), ("fl", fox_heads), ("rq", ret_width),
                        ("rk", ret_width), ("rv", ret_width), ("rg", ret_width),
                        ("gates", N_BRANCHES * d)):
        cuts[name] = slice(o, o + width)
        o += width
    assert o == w_in.shape[-1]
    w_in_t = jnp.swapaxes(w_in, 1, 2)
    tn = pool_width
    rows_of = lambda name: list(range(cuts[name].start, cuts[name].stop, tn))
    rows_a = sum((rows_of(n) for n in ("pool", "rq", "rk", "rg", "gates")), [])
    rows_b = sum((rows_of(n) for n in ("fq", "fk", "fv", "rv")), [])
    rq_blk, rk_blk, rg_blk = 1, 2, 3
    gate_blk = (pool_width + 3 * ret_width) // d
    rv_blk = 3

    bias = jnp.pad(forget_bias, ((0, 0), (0, LANES - fox_heads)))[:, None, :]
    row = lambda a: a[:, None, :]
    pool_w_b = pool_w.astype(BF16)
    branch_f32 = (w_branch_pool, w_branch_fox, w_branch_ret, w_out)
    branch_w = tuple(w[0].astype(BF16) for w in branch_f32)
    tables = _retention_tables(s, ret_heads)
    tq = 256

    halves = [(norm, w13, w2, l) for l in range(depth)
              for norm, w13, w2 in ((ffn1_norm, ffn1_w13, ffn1_w2), (ffn2_norm, ffn2_w13, ffn2_w2))]

    def half_step(xt, i, w13_b, w2_b):
        norm, _, _, l = halves[i]
        last = i == len(halves) - 1
        nxt = None if last else (halves[i + 1][1], halves[i + 1][2], halves[i + 1][3])
        return _ffn(xt, norm[l][None, :], w13_b, w2_b, next_weights=nxt,
                    final_gain=final_norm[None, :] if last else None)

    xt = x.reshape(b * s, d)
    w13_b, w2_b = ffn1_w13[0].astype(BF16), ffn1_w2[0].astype(BF16)
    for l in range(depth):
        xt, w13_b, w2_b = half_step(xt, 2 * l, w13_b, w2_b)
        gain = mix_norm[l][None, :]
        za, zb, fl = _in_proj(xt, gain, w_in_t, l, rows_a, rows_b, cuts["fl"].start, fox_heads, tn=tn)
        za3, zb3 = za.reshape(b, s, -1), zb.reshape(b, s, -1)
        cf, cft = _forget_cumsum(fl.reshape(b, s, LANES), bias, l, heads=fox_heads, tk=tq)
        y_fox = _fox(zb3, cf, cft, width=fox_width, tq=tq)
        y_ret = _retention(za3, zb3, tables, q_blk=rq_blk, k_blk=rk_blk, gate_blk=rg_blk,
                           v_blk=rv_blk, heads=ret_heads, tr=4 * CHUNK)
        y_pool = _pool(za3, pool_w_b, row(pool_scale), l, width=pool_width, ts=512)
        flat = lambda y: y.reshape(b * s, -1)
        xt, branch_w = _merge(flat(y_pool), flat(y_fox), flat(y_ret), za, xt, branch_w,
                              gate_blk=gate_blk,
                              next_weights=(*branch_f32, l + 1) if l + 1 < depth else None)
        xt, w13_b, w2_b = half_step(xt, 2 * l + 1, w13_b, w2_b)
    return xt.reshape(b, s, d)
```

```python
import functools

import jax
import jax.numpy as jnp
from jax import lax
from jax.experimental import pallas as pl
from jax.experimental.pallas import tpu as pltpu

F32 = jnp.float32
BF16 = jnp.bfloat16

RMS_EPS = 1e-6
GN_EPS = 1e-5
ROPE_BASE = 10000.0
POOL_WINDOWS = (2, 4, 8, 16)
POOL_HALO = 16
HEAD_DIM = 128
CHUNK = 128
N_BRANCHES = 3

LANES = 128
SUBLANES = 8
VMEM_LIMIT_BYTES = 60 * 1024 * 1024


def _params(*semantics):
    return pltpu.CompilerParams(dimension_semantics=semantics,
                                vmem_limit_bytes=VMEM_LIMIT_BYTES)


def _rms(x, g):
    return x * lax.rsqrt(jnp.mean(x * x, axis=-1, keepdims=True) + RMS_EPS) * g


def _silu(x):
    return x * jax.nn.sigmoid(x)


def _dot(a, b):
    return jnp.dot(a, b, preferred_element_type=F32)


def _dot_nt(a, b):
    return lax.dot_general(a, b, (((1,), (1,)), ((), ())), preferred_element_type=F32)


def _ffn_kernel(*refs, final_norm, convert_next):
    refs = list(refs)
    x_ref, g_ref, w1_ref, w3_ref, w2_ref = refs[:5]
    del refs[:5]
    gf_ref = refs.pop(0) if final_norm else None
    next_w13_ref, next_w2_ref = (refs.pop(0), refs.pop(0)) if convert_next else (None, None)
    o_ref = refs.pop(0)
    cast_w13_ref, cast_w2_ref = (refs.pop(0), refs.pop(0)) if convert_next else (None, None)
    (xn_ref,) = refs
    f = pl.program_id(1)

    @pl.when(f == 0)
    def _():
        x = x_ref[...]
        xn_ref[...] = _rms(x, g_ref[...]).astype(BF16)
        o_ref[...] = x

    xn = xn_ref[...]
    h = (_silu(_dot(xn, w1_ref[...])) * _dot(xn, w3_ref[...])).astype(BF16)
    o_ref[...] += 0.5 * _dot(h, w2_ref[...])

    if final_norm:
        @pl.when(f == pl.num_programs(1) - 1)
        def _():
            o_ref[...] = _rms(o_ref[...], gf_ref[...])

    if convert_next:
        cast_w13_ref[...] = next_w13_ref[...].astype(BF16)
        cast_w2_ref[...] = next_w2_ref[...].astype(BF16)


def _ffn(x, gain, w13, w2, *, final_gain=None, next_weights=None, tm=1024, tf=512):
    t, d = x.shape
    d_ff = w2.shape[0]
    gm, nf = t // tm, d_ff // tf
    in_specs = [
        pl.BlockSpec((tm, d), lambda m, f: (m, 0)),
        pl.BlockSpec((1, d), lambda m, f: (0, 0)),
        pl.BlockSpec((d, tf), lambda m, f: (0, f)),
        pl.BlockSpec((d, tf), lambda m, f: (0, f + nf)),
        pl.BlockSpec((tf, d), lambda m, f: (f, 0)),
    ]
    args = [x, gain, w13, w13, w2]
    out_specs = [pl.BlockSpec((tm, d), lambda m, f: (m, 0))]
    out_shape = [jax.ShapeDtypeStruct((t, d), F32)]
    if final_gain is not None:
        in_specs.append(pl.BlockSpec((1, d), lambda m, f: (0, 0)))
        args.append(final_gain)
    if next_weights is not None:
        n13, n2, nl = next_weights
        b13 = (d // gm, 2 * d_ff // nf)
        b2 = (d_ff // nf, d // gm)
        in_specs += [pl.BlockSpec((None,) + b13, lambda m, f: (nl, m, f)),
                     pl.BlockSpec((None,) + b2, lambda m, f: (nl, f, m))]
        args += [n13, n2]
        out_specs += [pl.BlockSpec(b13, lambda m, f: (m, f)), pl.BlockSpec(b2, lambda m, f: (f, m))]
        out_shape += [jax.ShapeDtypeStruct(n13.shape[1:], BF16),
                      jax.ShapeDtypeStruct(n2.shape[1:], BF16)]
    outs = pl.pallas_call(
        functools.partial(_ffn_kernel, final_norm=final_gain is not None,
                          convert_next=next_weights is not None),
        grid=(gm, nf),
        in_specs=in_specs,
        out_specs=out_specs,
        out_shape=out_shape,
        scratch_shapes=[pltpu.VMEM((tm, d), BF16)],
        compiler_params=_params("parallel", "arbitrary"),
        name="ffn",
    )(*args)
    return outs if next_weights is not None else (outs[0], None, None)


def _in_proj_kernel(rows_ref, x_ref, g_ref, w_ref, wg_ref, za_ref, zb_ref, fl_ref, xn_ref, *,
                    n_f32, gate_rows):
    del rows_ref
    j = pl.program_id(1)

    @pl.when(j == 0)
    def _():
        xn_ref[...] = _rms(x_ref[...], g_ref[...]).astype(BF16)
        row = lax.broadcasted_iota(jnp.int32, wg_ref.shape[1:], 0)
        wg = jnp.where(row < gate_rows, wg_ref[0], 0.0).astype(BF16)
        fl_ref[...] = _dot_nt(xn_ref[...], wg)

    @pl.when(j < n_f32)
    def _():
        za_ref[...] = _dot_nt(xn_ref[...], w_ref[0].astype(BF16))

    @pl.when(j >= n_f32)
    def _():
        zb_ref[...] = _dot_nt(xn_ref[...], w_ref[0].astype(BF16)).astype(zb_ref.dtype)


def _in_proj(x, gain, w_t, layer, rows_f32, rows_bf16, gate_start, gate_rows, *, tn, tm=1024):
    t, d = x.shape
    n_f32, n_bf16 = len(rows_f32), len(rows_bf16)
    gm, gn = t // tm, n_f32 + n_bf16
    assert all(r % SUBLANES == 0 for r in list(rows_f32) + list(rows_bf16) + [gate_start])
    table = jnp.asarray(list(rows_f32) + list(rows_bf16), jnp.int32) // SUBLANES
    return pl.pallas_call(
        functools.partial(_in_proj_kernel, n_f32=n_f32, gate_rows=gate_rows),
        grid_spec=pltpu.PrefetchScalarGridSpec(
            num_scalar_prefetch=1, grid=(gm, gn),
            in_specs=[
                pl.BlockSpec((tm, d), lambda m, j, tab: (m, 0)),
                pl.BlockSpec((1, d), lambda m, j, tab: (0, 0)),
                pl.BlockSpec((pl.Element(1), pl.Element(tn), pl.Element(d)),
                             lambda m, j, tab: (layer, tab[j] * SUBLANES, 0)),
                pl.BlockSpec((pl.Element(1), pl.Element(LANES), pl.Element(d)),
                             lambda m, j, tab: (layer, gate_start, 0)),
            ],
            out_specs=[
                pl.BlockSpec((tm, tn), lambda m, j, tab: (m, jnp.minimum(j, n_f32 - 1))),
                pl.BlockSpec((tm, tn), lambda m, j, tab: (m, jnp.maximum(j - n_f32, 0))),
                pl.BlockSpec((tm, LANES), lambda m, j, tab: (m, 0)),
            ],
            scratch_shapes=[pltpu.VMEM((tm, d), BF16)]),
        out_shape=[jax.ShapeDtypeStruct((t, n_f32 * tn), F32),
                   jax.ShapeDtypeStruct((t, n_bf16 * tn), BF16),
                   jax.ShapeDtypeStruct((t, LANES), F32)],
        compiler_params=_params("parallel", "arbitrary"),
        name="in_proj",
    )(table, x, gain, w_t, w_t)


def _forget_cumsum_kernel(fl_ref, b_ref, cf_ref, cft_ref, *, heads, tk):
    z = fl_ref[...] + b_ref[...]
    x = jnp.minimum(z, 0.0) - jnp.log1p(jnp.exp(-jnp.abs(z)))
    s = x.shape[0]
    row = lax.broadcasted_iota(jnp.int32, x.shape, 0)
    shift = 1
    while shift < s:
        if shift < SUBLANES:
            prev = jnp.where(row >= shift, pltpu.roll(x, shift, axis=0), 0.0)
        else:
            prev = jnp.concatenate([jnp.zeros((shift, x.shape[1]), F32), x[:s - shift]], axis=0)
        x = x + prev
        shift *= 2
    cf_ref[...] = x
    xt = x.T
    for j in range(s // tk):
        cft_ref[j] = xt[:heads, j * tk:(j + 1) * tk]


def _forget_cumsum(fl, bias, layer, *, heads, tk):
    b, s, w = fl.shape
    return pl.pallas_call(
        functools.partial(_forget_cumsum_kernel, heads=heads, tk=tk),
        grid=(b,),
        in_specs=[
            pl.BlockSpec((None, s, w), lambda i: (i, 0, 0)),
            pl.BlockSpec((None, 1, w), lambda i: (layer, 0, 0)),
        ],
        out_specs=[
            pl.BlockSpec((None, s, w), lambda i: (i, 0, 0)),
            pl.BlockSpec((None, s // tk, heads, tk), lambda i: (i, 0, 0, 0)),
        ],
        out_shape=[
            jax.ShapeDtypeStruct((b, s, w), F32),
            jax.ShapeDtypeStruct((b, s // tk, heads, tk), F32),
        ],
        compiler_params=_params("parallel"),
        name="forget_cumsum",
    )(fl, bias)


def _fox_kernel(q_ref, k_ref, v_ref, cf_ref, cft_ref, o_ref, fq_ref, m_ref, accl_ref, *, tq):
    i = pl.program_id(1)
    heads = q_ref.shape[-1] // HEAD_DIM
    log2e = 1.4426950408889634
    c1 = HEAD_DIM ** -0.5 * log2e
    causal = (lax.broadcasted_iota(jnp.int32, (tq, tq), 0)
              >= lax.broadcasted_iota(jnp.int32, (tq, tq), 1))
    ones = jnp.ones((tq, HEAD_DIM), BF16)
    wide = lambda a, lanes: jnp.concatenate([a] * (lanes // LANES), axis=1)

    cf = cf_ref[...] * log2e
    for h in range(heads):
        fq_ref[h] = jnp.broadcast_to(cf[:, h:h + 1], (tq, LANES))
    m_ref[...] = jnp.full(m_ref.shape, -jnp.inf, F32)
    accl_ref[...] = jnp.zeros(accl_ref.shape, F32)

    def block(j, masked):
        k0 = pl.multiple_of(j * tq, tq)
        for h in range(heads):
            lanes = slice(h * HEAD_DIM, (h + 1) * HEAD_DIM)
            kb = k_ref[pl.ds(k0, tq), lanes]
            v1 = jnp.concatenate([v_ref[pl.ds(k0, tq), lanes], ones], axis=1)
            fk = cft_ref[j, h:h + 1, :] * log2e
            t = _dot_nt(q_ref[:, lanes], kb) * c1 + wide(fq_ref[h], tq) - fk
            if masked:
                t = jnp.where(causal, t, -jnp.inf)
            m_prev = m_ref[h]
            m_new = jnp.maximum(m_prev, jnp.max(t, axis=-1, keepdims=True))
            p = jnp.exp2(t - wide(m_new, tq))
            alpha = wide(jnp.exp2(m_prev - m_new), 2 * HEAD_DIM)
            accl_ref[h] = alpha * accl_ref[h] + _dot(p.astype(BF16), v1)
            m_ref[h] = m_new

    lax.fori_loop(0, i, lambda j, c: (block(j, False), c)[1], 0)
    block(i, True)
    for h in range(heads):
        accl = accl_ref[h]
        o_ref[:, h * HEAD_DIM:(h + 1) * HEAD_DIM] = (
            accl[:, :HEAD_DIM] / accl[:, HEAD_DIM:]).astype(o_ref.dtype)


def _fox(zb, cf, cft, *, width, tq):
    b, s, _ = zb.shape
    heads = width // HEAD_DIM
    return pl.pallas_call(
        functools.partial(_fox_kernel, tq=tq),
        grid=(b, s // tq),
        in_specs=[
            pl.BlockSpec((None, tq, width), lambda bi, i: (bi, i, 0)),
            pl.BlockSpec((None, s, width), lambda bi, i: (bi, 0, 1)),
            pl.BlockSpec((None, s, width), lambda bi, i: (bi, 0, 2)),
            pl.BlockSpec((None, tq, LANES), lambda bi, i: (bi, i, 0)),
            pl.BlockSpec((None, s // tq, heads, tq), lambda bi, i: (bi, 0, 0, 0)),
        ],
        out_specs=pl.BlockSpec((None, tq, width), lambda bi, i: (bi, i, 0)),
        out_shape=jax.ShapeDtypeStruct((b, s, width), BF16),
        scratch_shapes=[pltpu.VMEM((heads, tq, LANES), F32),
                        pltpu.VMEM((heads, tq, LANES), F32),
                        pltpu.VMEM((heads, tq, 2 * HEAD_DIM), F32)],
        compiler_params=_params("parallel", "arbitrary"),
        name="fox",
    )(zb, zb, zb, cf, cft)


def _ret_kernel(q_ref, k_ref, v_ref, gate_ref, cos_ref, sin_ref, dm_ref, xi_ref, zeta_ref,
                cd_ref, o_ref, state_ref, *, chunks):
    heads = state_ref.shape[0]

    @pl.when(pl.program_id(1) == 0)
    def _():
        state_ref[...] = jnp.zeros_like(state_ref)

    for c in range(chunks):
        r = slice(c * CHUNK, (c + 1) * CHUNK)
        cos, sin = cos_ref[r, :], sin_ref[r, :]
        for h in range(heads):
            lanes = slice(h * HEAD_DIM, (h + 1) * HEAD_DIM)
            q, k = q_ref[r, lanes], k_ref[r, lanes]
            q = q * cos + pltpu.roll(q, HEAD_DIM // 2, axis=1) * sin
            k = (k * cos + pltpu.roll(k, HEAD_DIM // 2, axis=1) * sin) * (HEAD_DIM ** -0.5)
            v = v_ref[r, lanes]
            state = state_ref[h]
            scores = _dot_nt(q.astype(BF16), k.astype(BF16)) * dm_ref[h]
            o = _dot(scores.astype(BF16), v)
            o = o + _dot((q * xi_ref[h]).astype(BF16), state.astype(BF16))
            kz_t = (k * zeta_ref[h]).T.astype(BF16)
            state_ref[h] = cd_ref[h, 0:1, :] * state + _dot(kz_t, v)
            mu = jnp.mean(o, axis=-1, keepdims=True)
            var = jnp.mean(jnp.square(o - mu), axis=-1, keepdims=True)
            o = (o - mu) * lax.rsqrt(var + GN_EPS)
            o_ref[r, lanes] = (_silu(gate_ref[r, lanes]) * o).astype(o_ref.dtype)


def _retention(za, zb, tables, *, q_blk, k_blk, gate_blk, v_blk, heads, tr):
    b, s, _ = za.shape
    width = heads * HEAD_DIM
    cos, sin, dmat, xi, zeta, cdec = tables
    tok = lambda off: pl.BlockSpec((None, tr, width), lambda bi, i: (bi, i, off))
    whole = lambda a: pl.BlockSpec(a.shape, lambda bi, i: (0, 0, 0))
    pos = pl.BlockSpec((tr, HEAD_DIM), lambda bi, i: (i, 0))
    return pl.pallas_call(
        functools.partial(_ret_kernel, chunks=tr // CHUNK),
        grid=(b, s // tr),
        in_specs=[tok(q_blk), tok(k_blk), tok(v_blk), tok(gate_blk), pos, pos,
                  whole(dmat), whole(xi), whole(zeta), whole(cdec)],
        out_specs=pl.BlockSpec((None, tr, width), lambda bi, i: (bi, i, 0)),
        out_shape=jax.ShapeDtypeStruct((b, s, width), BF16),
        scratch_shapes=[pltpu.VMEM((heads, HEAD_DIM, HEAD_DIM), F32)],
        compiler_params=_params("parallel", "arbitrary"),
        name="retention",
    )(za, za, zb, za, cos, sin, dmat, xi, zeta, cdec)


def _retention_tables(s, heads):
    half = HEAD_DIM // 2
    inv_freq = ROPE_BASE ** (-jnp.arange(half, dtype=F32) / half)
    ang = jnp.arange(s, dtype=F32)[:, None] * inv_freq[None, :]
    cos = jnp.concatenate([jnp.cos(ang), jnp.cos(ang)], axis=-1)
    sin = jnp.concatenate([-jnp.sin(ang), jnp.sin(ang)], axis=-1)
    log_gamma = jnp.log1p(-jnp.power(2.0, -5.0 - jnp.arange(heads, dtype=F32)))
    pos = jnp.arange(CHUNK, dtype=F32)
    diff = pos[:, None] - pos[None, :]
    dmat = jnp.where(diff[None] >= 0,
                     jnp.exp(jnp.maximum(diff, 0.0)[None] * log_gamma[:, None, None]), 0.0)
    xi = jnp.exp((pos + 1.0)[None, :] * log_gamma[:, None])
    zeta = jnp.exp((CHUNK - 1.0 - pos)[None, :] * log_gamma[:, None])
    cdec = jnp.exp(CHUNK * log_gamma)
    wide = lambda a: jnp.broadcast_to(a[:, :, None], (heads, CHUNK, HEAD_DIM))
    return (cos, sin, dmat, wide(xi), wide(zeta),
            jnp.broadcast_to(cdec[:, None, None], (heads, SUBLANES, HEAD_DIM)))


def _pool_kernel(u_ref, w_ref, sc_ref, o_ref, ext_ref, *, ts):
    s = pl.program_id(1)
    gd = u_ref.shape[-1] // len(POOL_WINDOWS)

    @pl.when(s == 0)
    def _():
        ext_ref[0:POOL_HALO, :] = jnp.zeros((POOL_HALO, ext_ref.shape[1]), F32)

    @pl.when(s > 0)
    def _():
        ext_ref[0:POOL_HALO, :] = ext_ref[ts:ts + POOL_HALO, :]

    ext_ref[POOL_HALO:POOL_HALO + ts, :] = u_ref[...]
    t = s * ts + lax.broadcasted_iota(jnp.int32, (ts, 1), 0)
    for g, w in enumerate(POOL_WINDOWS):
        cols = slice(g * gd, (g + 1) * gd)
        u = ext_ref[POOL_HALO:POOL_HALO + ts, cols]
        win = u
        for k in range(1, w):
            win = win + ext_ref[POOL_HALO - k:POOL_HALO - k + ts, cols]
        cnt = jnp.minimum(t + 1, w).astype(F32)
        pooled = win / cnt - u
        mixed = _dot(pooled.astype(BF16), w_ref[g])
        o_ref[:, cols] = (mixed * sc_ref[:, cols]).astype(o_ref.dtype)


def _pool(za, pool_w, pool_scale, layer, *, width, ts):
    b, s, _ = za.shape
    groups, gd = pool_w.shape[1], pool_w.shape[2]
    return pl.pallas_call(
        functools.partial(_pool_kernel, ts=ts),
        grid=(b, s // ts),
        in_specs=[
            pl.BlockSpec((None, ts, width), lambda bi, i: (bi, i, 0)),
            pl.BlockSpec((None, groups, gd, gd), lambda bi, i: (layer, 0, 0, 0)),
            pl.BlockSpec((None, 1, width), lambda bi, i: (layer, 0, 0)),
        ],
        out_specs=pl.BlockSpec((None, ts, width), lambda bi, i: (bi, i, 0)),
        out_shape=jax.ShapeDtypeStruct((b, s, width), BF16),
        scratch_shapes=[pltpu.VMEM((POOL_HALO + ts, width), F32)],
        compiler_params=_params("parallel", "arbitrary"),
        name="pool",
    )(za, pool_w, pool_scale)


def _merge_kernel(*refs, convert_next):
    yp_ref, yf_ref, yr_ref, g0_ref, g1_ref, g2_ref, x_ref, wp_ref, wf_ref, wr_ref, wo_ref = refs[:11]
    n = 4 if convert_next else 0
    next_refs, o_ref, cast_refs = refs[11:11 + n], refs[11 + n], refs[12 + n:]
    merged = (jax.nn.sigmoid(g0_ref[...]) * _dot(yp_ref[...], wp_ref[...])
              + jax.nn.sigmoid(g1_ref[...]) * _dot(yf_ref[...], wf_ref[...])
              + jax.nn.sigmoid(g2_ref[...]) * _dot(yr_ref[...], wr_ref[...]))
    o_ref[...] = x_ref[...] + _dot(merged.astype(BF16), wo_ref[...])
    for src_ref, dst_ref in zip(next_refs, cast_refs):
        dst_ref[...] = src_ref[...].astype(BF16)


def _merge(y_pool, y_fox, y_ret, za, x, weights, *, gate_blk, next_weights=None, tm=256):
    t, d = x.shape
    gm = t // tm
    branch = lambda y: pl.BlockSpec((tm, y.shape[-1]), lambda m: (m, 0))
    gate = lambda i: pl.BlockSpec((tm, d), lambda m: (m, gate_blk + i))
    resident = lambda w: pl.BlockSpec(w.shape, lambda m: (0, 0), pipeline_mode=pl.Buffered(1))
    in_specs = [branch(y_pool), branch(y_fox), branch(y_ret), gate(0), gate(1), gate(2),
                pl.BlockSpec((tm, d), lambda m: (m, 0))] + [resident(w) for w in weights]
    args = [y_pool, y_fox, y_ret, za, za, za, x, *weights]
    out_specs = [pl.BlockSpec((tm, d), lambda m: (m, 0))]
    out_shape = [jax.ShapeDtypeStruct((t, d), F32)]
    if next_weights is not None:
        *stacked, nl = next_weights
        for w in stacked:
            rows = w.shape[1] // gm
            in_specs.append(pl.BlockSpec((None, rows, w.shape[2]), lambda m: (nl, m, 0)))
            out_specs.append(pl.BlockSpec((rows, w.shape[2]), lambda m: (m, 0)))
            out_shape.append(jax.ShapeDtypeStruct(w.shape[1:], BF16))
        args += stacked
    outs = pl.pallas_call(
        functools.partial(_merge_kernel, convert_next=next_weights is not None),
        grid=(gm,),
        in_specs=in_specs,
        out_specs=out_specs,
        out_shape=out_shape,
        compiler_params=_params("parallel"),
        name="merge",
    )(*args)
    return outs[0], (tuple(outs[1:]) if next_weights is not None else None)


def kernel(x, ffn1_norm, ffn1_w13, ffn1_w2, mix_norm, w_in, forget_bias, pool_w, pool_scale,
           w_branch_pool, w_branch_fox, w_branch_ret, w_out, ffn2_norm, ffn2_w13, ffn2_w2,
           final_norm):
    b, s, d = x.shape
    depth = w_in.shape[0]
    pool_width = pool_scale.shape[-1]
    fox_heads = forget_bias.shape[-1]
    fox_width = w_branch_fox.shape[1]
    ret_width = w_branch_ret.shape[1]
    ret_heads = ret_width // HEAD_DIM
    assert fox_width == fox_heads * HEAD_DIM and pool_width == fox_width == ret_width
    assert d % fox_width == 0

    o = 0
    cuts = {}
    for name, width in (("pool", pool_width), ("fq", fox_width), ("fk", fox_width),
                        ("fv", fox_width), ("fl", fox_heads), ("rq", ret_width),
                        ("rk", ret_width), ("rv", ret_width), ("rg", ret_width),
                        ("gates", N_BRANCHES * d)):
        cuts[name] = slice(o, o + width)
        o += width
    assert o == w_in.shape[-1]
    w_in_t = jnp.swapaxes(w_in, 1, 2)
    tn = pool_width
    rows_of = lambda name: list(range(cuts[name].start, cuts[name].stop, tn))
    rows_a = sum((rows_of(n) for n in ("pool", "rq", "rk", "rg", "gates")), [])
    rows_b = sum((rows_of(n) for n in ("fq", "fk", "fv", "rv")), [])
    rq_blk, rk_blk, rg_blk = 1, 2, 3
    gate_blk = (pool_width + 3 * ret_width) // d
    rv_blk = 3

    bias = jnp.pad(forget_bias, ((0, 0), (0, LANES - fox_heads)))[:, None, :]
    row = lambda a: a[:, None, :]
    pool_w_b = pool_w.astype(BF16)
    branch_f32 = (w_branch_pool, w_branch_fox, w_branch_ret, w_out)
    branch_w = tuple(w[0].astype(BF16) for w in branch_f32)
    tables = _retention_tables(s, ret_heads)
    tq = 512

    halves = [(norm, w13, w2, l) for l in range(depth)
              for norm, w13, w2 in ((ffn1_norm, ffn1_w13, ffn1_w2), (ffn2_norm, ffn2_w13, ffn2_w2))]

    def half_step(xt, i, w13_b, w2_b):
        norm, _, _, l = halves[i]
        last = i == len(halves) - 1
        nxt = None if last else (halves[i + 1][1], halves[i + 1][2], halves[i + 1][3])
        return _ffn(xt, norm[l][None, :], w13_b, w2_b, next_weights=nxt,
                    final_gain=final_norm[None, :] if last else None)

    xt = x.reshape(b * s, d)
    w13_b, w2_b = ffn1_w13[0].astype(BF16), ffn1_w2[0].astype(BF16)
    for l in range(depth):
        xt, w13_b, w2_b = half_step(xt, 2 * l, w13_b, w2_b)
        gain = mix_norm[l][None, :]
        za, zb, fl = _in_proj(xt, gain, w_in_t, l, rows_a, rows_b, cuts["fl"].start, fox_heads, tn=tn)
        za3, zb3 = za.reshape(b, s, -1), zb.reshape(b, s, -1)
        cf, cft = _forget_cumsum(fl.reshape(b, s, LANES), bias, l, heads=fox_heads, tk=tq)
        y_fox = _fox(zb3, cf, cft, width=fox_width, tq=tq)
        y_ret = _retention(za3, zb3, tables, q_blk=rq_blk, k_blk=rk_blk, gate_blk=rg_blk,
                           v_blk=rv_blk, heads=ret_heads, tr=4 * CHUNK)
        y_pool = _pool(za3, pool_w_b, row(pool_scale), l, width=pool_width, ts=512)
        flat = lambda y: y.reshape(b * s, -1)
        xt, branch_w = _merge(flat(y_pool), flat(y_fox), flat(y_ret), za, xt, branch_w,
                              gate_blk=gate_blk,
                              next_weights=(*branch_f32, l + 1) if l + 1 < depth else None)
        xt, w13_b, w2_b = half_step(xt, 2 * l + 1, w13_b, w2_b)
    return xt.reshape(b, s, d)
```

```python
import functools

import jax
import jax.numpy as jnp
from jax import lax
from jax.experimental import pallas as pl
from jax.experimental.pallas import tpu as pltpu

F32 = jnp.float32
BF16 = jnp.bfloat16

RMS_EPS = 1e-6
GN_EPS = 1e-5
ROPE_BASE = 10000.0
POOL_WINDOWS = (2, 4, 8, 16)
POOL_HALO = 16
HEAD_DIM = 128
CHUNK = 128
N_BRANCHES = 3

LANES = 128
SUBLANES = 8
VMEM_LIMIT_BYTES = 60 * 1024 * 1024

FFN_TM, FFN_TF = 1024, 512
IN_PROJ_TM = 1024
MERGE_TM = 256
FOX_TQ = 512
RET_POOL_TS = 4 * CHUNK


def _params(*semantics):
    return pltpu.CompilerParams(dimension_semantics=semantics,
                                vmem_limit_bytes=VMEM_LIMIT_BYTES)


def _rms(x, g):
    return x * lax.rsqrt(jnp.mean(x * x, axis=-1, keepdims=True) + RMS_EPS) * g


def _silu(x):
    return x * jax.nn.sigmoid(x)


def _dot(a, b):
    return jnp.dot(a, b, preferred_element_type=F32)


def _dot_nt(a, b):
    return lax.dot_general(a, b, (((1,), (1,)), ((), ())), preferred_element_type=F32)


def _ffn_kernel(*refs, final_norm, convert_next):
    refs = list(refs)
    x_ref, g_ref, w1_ref, w3_ref, w2_ref = refs[:5]
    del refs[:5]
    gf_ref = refs.pop(0) if final_norm else None
    next_w13_ref, next_w2_ref = (refs.pop(0), refs.pop(0)) if convert_next else (None, None)
    o_ref = refs.pop(0)
    cast_w13_ref, cast_w2_ref = (refs.pop(0), refs.pop(0)) if convert_next else (None, None)
    (xn_ref,) = refs
    f = pl.program_id(1)

    @pl.when(f == 0)
    def _():
        x = x_ref[...]
        xn_ref[...] = _rms(x, g_ref[...]).astype(BF16)
        o_ref[...] = x

    xn = xn_ref[...]
    h = (_silu(_dot(xn, w1_ref[...])) * _dot(xn, w3_ref[...])).astype(BF16)
    o_ref[...] += 0.5 * _dot(h, w2_ref[...])

    if final_norm:
        @pl.when(f == pl.num_programs(1) - 1)
        def _():
            o_ref[...] = _rms(o_ref[...], gf_ref[...])

    if convert_next:
        cast_w13_ref[...] = next_w13_ref[...].astype(BF16)
        cast_w2_ref[...] = next_w2_ref[...].astype(BF16)


def _ffn(x, gain, w13, w2, *, final_gain=None, next_weights=None, tm=FFN_TM, tf=FFN_TF):
    t, d = x.shape
    d_ff = w2.shape[0]
    gm, nf = t // tm, d_ff // tf
    in_specs = [
        pl.BlockSpec((tm, d), lambda m, f: (m, 0)),
        pl.BlockSpec((1, d), lambda m, f: (0, 0)),
        pl.BlockSpec((d, tf), lambda m, f: (0, f)),
        pl.BlockSpec((d, tf), lambda m, f: (0, f + nf)),
        pl.BlockSpec((tf, d), lambda m, f: (f, 0)),
    ]
    args = [x, gain, w13, w13, w2]
    out_specs = [pl.BlockSpec((tm, d), lambda m, f: (m, 0))]
    out_shape = [jax.ShapeDtypeStruct((t, d), F32)]
    if final_gain is not None:
        in_specs.append(pl.BlockSpec((1, d), lambda m, f: (0, 0)))
        args.append(final_gain)
    if next_weights is not None:
        n13, n2, nl = next_weights
        b13 = (d // gm, 2 * d_ff // nf)
        b2 = (d_ff // nf, d // gm)
        in_specs += [pl.BlockSpec((None,) + b13, lambda m, f: (nl, m, f)),
                     pl.BlockSpec((None,) + b2, lambda m, f: (nl, f, m))]
        args += [n13, n2]
        out_specs += [pl.BlockSpec(b13, lambda m, f: (m, f)), pl.BlockSpec(b2, lambda m, f: (f, m))]
        out_shape += [jax.ShapeDtypeStruct(n13.shape[1:], BF16),
                      jax.ShapeDtypeStruct(n2.shape[1:], BF16)]
    outs = pl.pallas_call(
        functools.partial(_ffn_kernel, final_norm=final_gain is not None,
                          convert_next=next_weights is not None),
        grid=(gm, nf),
        in_specs=in_specs,
        out_specs=out_specs,
        out_shape=out_shape,
        scratch_shapes=[pltpu.VMEM((tm, d), BF16)],
        compiler_params=_params("parallel", "arbitrary"),
        name="ffn",
    )(*args)
    return outs if next_weights is not None else (outs[0], None, None)


def _in_proj_kernel(rows_ref, x_ref, g_ref, w_ref, wg_ref, za_ref, zb_ref, fl_ref, xn_ref, *,
                    n_f32, gate_rows):
    del rows_ref
    j = pl.program_id(1)

    @pl.when(j == 0)
    def _():
        xn_ref[...] = _rms(x_ref[...], g_ref[...]).astype(BF16)
        row = lax.broadcasted_iota(jnp.int32, wg_ref.shape[1:], 0)
        wg = jnp.where(row < gate_rows, wg_ref[0], 0.0).astype(BF16)
        fl_ref[...] = _dot_nt(xn_ref[...], wg)

    @pl.when(j < n_f32)
    def _():
        za_ref[...] = _dot_nt(xn_ref[...], w_ref[0].astype(BF16))

    @pl.when(j >= n_f32)
    def _():
        zb_ref[...] = _dot_nt(xn_ref[...], w_ref[0].astype(BF16)).astype(zb_ref.dtype)


def _in_proj(x, gain, w_t, layer, rows_f32, rows_bf16, gate_start, gate_rows, *, tn,
             tm=IN_PROJ_TM):
    t, d = x.shape
    n_f32, n_bf16 = len(rows_f32), len(rows_bf16)
    gm, gn = t // tm, n_f32 + n_bf16
    assert all(r % SUBLANES == 0 for r in list(rows_f32) + list(rows_bf16) + [gate_start])
    table = jnp.asarray(list(rows_f32) + list(rows_bf16), jnp.int32) // SUBLANES
    return pl.pallas_call(
        functools.partial(_in_proj_kernel, n_f32=n_f32, gate_rows=gate_rows),
        grid_spec=pltpu.PrefetchScalarGridSpec(
            num_scalar_prefetch=1, grid=(gm, gn),
            in_specs=[
                pl.BlockSpec((tm, d), lambda m, j, tab: (m, 0)),
                pl.BlockSpec((1, d), lambda m, j, tab: (0, 0)),
                pl.BlockSpec((pl.Element(1), pl.Element(tn), pl.Element(d)),
                             lambda m, j, tab: (layer, tab[j] * SUBLANES, 0)),
                pl.BlockSpec((pl.Element(1), pl.Element(LANES), pl.Element(d)),
                             lambda m, j, tab: (layer, gate_start, 0)),
            ],
            out_specs=[
                pl.BlockSpec((tm, tn), lambda m, j, tab: (m, jnp.minimum(j, n_f32 - 1))),
                pl.BlockSpec((tm, tn), lambda m, j, tab: (m, jnp.maximum(j - n_f32, 0))),
                pl.BlockSpec((tm, LANES), lambda m, j, tab: (m, 0)),
            ],
            scratch_shapes=[pltpu.VMEM((tm, d), BF16)]),
        out_shape=[jax.ShapeDtypeStruct((t, n_f32 * tn), F32),
                   jax.ShapeDtypeStruct((t, n_bf16 * tn), BF16),
                   jax.ShapeDtypeStruct((t, LANES), F32)],
        compiler_params=_params("parallel", "arbitrary"),
        name="in_proj",
    )(table, x, gain, w_t, w_t)


def _forget_cumsum_kernel(fl_ref, b_ref, cf_ref, cft_ref, *, heads, tk):
    z = fl_ref[...] + b_ref[...]
    x = jnp.minimum(z, 0.0) - jnp.log1p(jnp.exp(-jnp.abs(z)))
    s = x.shape[0]
    row = lax.broadcasted_iota(jnp.int32, x.shape, 0)
    shift = 1
    while shift < s:
        if shift < SUBLANES:
            prev = jnp.where(row >= shift, pltpu.roll(x, shift, axis=0), 0.0)
        else:
            prev = jnp.concatenate([jnp.zeros((shift, x.shape[1]), F32), x[:s - shift]], axis=0)
        x = x + prev
        shift *= 2
    cf_ref[...] = x
    xt = x.T
    for j in range(s // tk):
        cft_ref[j] = xt[:heads, j * tk:(j + 1) * tk]


def _forget_cumsum(fl, bias, layer, *, heads, tk):
    b, s, w = fl.shape
    return pl.pallas_call(
        functools.partial(_forget_cumsum_kernel, heads=heads, tk=tk),
        grid=(b,),
        in_specs=[
            pl.BlockSpec((None, s, w), lambda i: (i, 0, 0)),
            pl.BlockSpec((None, 1, w), lambda i: (layer, 0, 0)),
        ],
        out_specs=[
            pl.BlockSpec((None, s, w), lambda i: (i, 0, 0)),
            pl.BlockSpec((None, s // tk, heads, tk), lambda i: (i, 0, 0, 0)),
        ],
        out_shape=[
            jax.ShapeDtypeStruct((b, s, w), F32),
            jax.ShapeDtypeStruct((b, s // tk, heads, tk), F32),
        ],
        compiler_params=_params("parallel"),
        name="forget_cumsum",
    )(fl, bias)


def _fox_kernel(q_ref, k_ref, v_ref, cf_ref, cft_ref, o_ref, fq_ref, m_ref, accl_ref, *, tq):
    i = pl.program_id(1)
    heads = q_ref.shape[-1] // HEAD_DIM
    log2e = 1.4426950408889634
    c1 = HEAD_DIM ** -0.5 * log2e
    causal = (lax.broadcasted_iota(jnp.int32, (tq, tq), 0)
              >= lax.broadcasted_iota(jnp.int32, (tq, tq), 1))
    ones = jnp.ones((tq, HEAD_DIM), BF16)
    wide = lambda a, lanes: jnp.concatenate([a] * (lanes // LANES), axis=1)

    cf = cf_ref[...] * log2e
    for h in range(heads):
        fq_ref[h] = jnp.broadcast_to(cf[:, h:h + 1], (tq, LANES))
    m_ref[...] = jnp.full(m_ref.shape, -jnp.inf, F32)
    accl_ref[...] = jnp.zeros(accl_ref.shape, F32)

    def block(j, masked):
        k0 = pl.multiple_of(j * tq, tq)
        for h in range(heads):
            lanes = slice(h * HEAD_DIM, (h + 1) * HEAD_DIM)
            kb = k_ref[pl.ds(k0, tq), lanes]
            v1 = jnp.concatenate([v_ref[pl.ds(k0, tq), lanes], ones], axis=1)
            fk = cft_ref[j, h:h + 1, :] * log2e
            t = _dot_nt(q_ref[:, lanes], kb) * c1 + wide(fq_ref[h], tq) - fk
            if masked:
                t = jnp.where(causal, t, -jnp.inf)
            m_prev = m_ref[h]
            m_new = jnp.maximum(m_prev, jnp.max(t, axis=-1, keepdims=True))
            p = jnp.exp2(t - wide(m_new, tq))
            alpha = wide(jnp.exp2(m_prev - m_new), 2 * HEAD_DIM)
            accl_ref[h] = alpha * accl_ref[h] + _dot(p.astype(BF16), v1)
            m_ref[h] = m_new

    lax.fori_loop(0, i, lambda j, c: (block(j, False), c)[1], 0)
    block(i, True)
    for h in range(heads):
        accl = accl_ref[h]
        o_ref[:, h * HEAD_DIM:(h + 1) * HEAD_DIM] = (
            accl[:, :HEAD_DIM] / accl[:, HEAD_DIM:]).astype(o_ref.dtype)


def _fox(zb, cf, cft, *, width, tq):
    b, s, _ = zb.shape
    heads = width // HEAD_DIM
    return pl.pallas_call(
        functools.partial(_fox_kernel, tq=tq),
        grid=(b, s // tq),
        in_specs=[
            pl.BlockSpec((None, tq, width), lambda bi, i: (bi, i, 0)),
            pl.BlockSpec((None, s, width), lambda bi, i: (bi, 0, 1)),
            pl.BlockSpec((None, s, width), lambda bi, i: (bi, 0, 2)),
            pl.BlockSpec((None, tq, LANES), lambda bi, i: (bi, i, 0)),
            pl.BlockSpec((None, s // tq, heads, tq), lambda bi, i: (bi, 0, 0, 0)),
        ],
        out_specs=pl.BlockSpec((None, tq, width), lambda bi, i: (bi, i, 0)),
        out_shape=jax.ShapeDtypeStruct((b, s, width), BF16),
        scratch_shapes=[pltpu.VMEM((heads, tq, LANES), F32),
                        pltpu.VMEM((heads, tq, LANES), F32),
                        pltpu.VMEM((heads, tq, 2 * HEAD_DIM), F32)],
        compiler_params=_params("parallel", "arbitrary"),
        name="fox",
    )(zb, zb, zb, cf, cft)


def _retention_tile(q_ref, k_ref, v_ref, gate_ref, cos_ref, sin_ref, dm_ref, xi_ref, zeta_ref,
                    cd_ref, o_ref, state_ref):
    heads = state_ref.shape[0]
    for c in range(q_ref.shape[0] // CHUNK):
        r = slice(c * CHUNK, (c + 1) * CHUNK)
        cos, sin = cos_ref[r, :], sin_ref[r, :]
        for h in range(heads):
            lanes = slice(h * HEAD_DIM, (h + 1) * HEAD_DIM)
            q, k = q_ref[r, lanes], k_ref[r, lanes]
            q = q * cos + pltpu.roll(q, HEAD_DIM // 2, axis=1) * sin
            k = (k * cos + pltpu.roll(k, HEAD_DIM // 2, axis=1) * sin) * (HEAD_DIM ** -0.5)
            v = v_ref[r, lanes]
            state = state_ref[h]
            scores = _dot_nt(q.astype(BF16), k.astype(BF16)) * dm_ref[h]
            o = _dot(scores.astype(BF16), v)
            o = o + _dot((q * xi_ref[h]).astype(BF16), state.astype(BF16))
            kz_t = (k * zeta_ref[h]).T.astype(BF16)
            state_ref[h] = cd_ref[h, 0:1, :] * state + _dot(kz_t, v)
            mu = jnp.mean(o, axis=-1, keepdims=True)
            var = jnp.mean(jnp.square(o - mu), axis=-1, keepdims=True)
            o = (o - mu) * lax.rsqrt(var + GN_EPS)
            o_ref[r, lanes] = (_silu(gate_ref[r, lanes]) * o).astype(o_ref.dtype)


def _pool_tile(u_ref, w_ref, sc_ref, o_ref, ext_ref, seq_tile):
    ts = u_ref.shape[0]
    gd = u_ref.shape[-1] // len(POOL_WINDOWS)
    ext_ref[POOL_HALO:POOL_HALO + ts, :] = u_ref[...]
    t = seq_tile * ts + lax.broadcasted_iota(jnp.int32, (ts, 1), 0)
    for g, w in enumerate(POOL_WINDOWS):
        cols = slice(g * gd, (g + 1) * gd)
        u = ext_ref[POOL_HALO:POOL_HALO + ts, cols]
        win = u
        for k in range(1, w):
            win = win + ext_ref[POOL_HALO - k:POOL_HALO - k + ts, cols]
        cnt = jnp.minimum(t + 1, w).astype(F32)
        pooled = win / cnt - u
        mixed = _dot(pooled.astype(BF16), w_ref[g])
        o_ref[:, cols] = (mixed * sc_ref[:, cols]).astype(o_ref.dtype)


def _ret_pool_kernel(u_ref, q_ref, k_ref, gate_ref, v_ref, cos_ref, sin_ref, dm_ref, xi_ref, zeta_ref,
                     cd_ref, pw_ref, psc_ref, yr_ref, yp_ref, state_ref, ext_ref):
    i = pl.program_id(1)
    ts = u_ref.shape[0]

    @pl.when(i == 0)
    def _():
        state_ref[...] = jnp.zeros_like(state_ref)
        ext_ref[0:POOL_HALO, :] = jnp.zeros((POOL_HALO, ext_ref.shape[1]), F32)

    @pl.when(i > 0)
    def _():
        ext_ref[0:POOL_HALO, :] = ext_ref[ts:ts + POOL_HALO, :]

    _pool_tile(u_ref, pw_ref, psc_ref, yp_ref, ext_ref, i)
    _retention_tile(q_ref, k_ref, v_ref, gate_ref, cos_ref, sin_ref, dm_ref, xi_ref, zeta_ref,
                    cd_ref, yr_ref, state_ref)


def _retention_and_pool(za, zb, tables, pool_w, pool_scale, layer, *, width, ts):
    b, s, _ = za.shape
    heads = width // HEAD_DIM
    cos, sin, dmat, xi, zeta, cdec = tables
    tok = lambda blk: pl.BlockSpec((None, ts, width), lambda bi, i: (bi, i, blk))
    whole = lambda a: pl.BlockSpec(a.shape, lambda bi, i: (0, 0, 0))
    pos = pl.BlockSpec((ts, HEAD_DIM), lambda bi, i: (i, 0))
    groups, gd = pool_w.shape[1], pool_w.shape[2]
    out = jax.ShapeDtypeStruct((b, s, width), BF16)
    return pl.pallas_call(
        _ret_pool_kernel,
        grid=(b, s // ts),
        in_specs=[tok(0), tok(1), tok(2), tok(3), tok(3), pos, pos,
                  whole(dmat), whole(xi), whole(zeta), whole(cdec),
                  pl.BlockSpec((None, groups, gd, gd), lambda bi, i: (layer, 0, 0, 0)),
                  pl.BlockSpec((None, 1, width), lambda bi, i: (layer, 0, 0))],
        out_specs=[tok(0), tok(0)],
        out_shape=[out, out],
        scratch_shapes=[pltpu.VMEM((heads, HEAD_DIM, HEAD_DIM), F32),
                        pltpu.VMEM((POOL_HALO + ts, width), F32)],
        compiler_params=_params("parallel", "arbitrary"),
        name="retention_pool",
    )(za, za, za, za, zb, cos, sin, dmat, xi, zeta, cdec, pool_w, pool_scale)


def _retention_tables(s, heads):
    half = HEAD_DIM // 2
    inv_freq = ROPE_BASE ** (-jnp.arange(half, dtype=F32) / half)
    ang = jnp.arange(s, dtype=F32)[:, None] * inv_freq[None, :]
    cos = jnp.concatenate([jnp.cos(ang), jnp.cos(ang)], axis=-1)
    sin = jnp.concatenate([-jnp.sin(ang), jnp.sin(ang)], axis=-1)
    log_gamma = jnp.log1p(-jnp.power(2.0, -5.0 - jnp.arange(heads, dtype=F32)))
    pos = jnp.arange(CHUNK, dtype=F32)
    diff = pos[:, None] - pos[None, :]
    dmat = jnp.where(diff[None] >= 0,
                     jnp.exp(jnp.maximum(diff, 0.0)[None] * log_gamma[:, None, None]), 0.0)
    xi = jnp.exp((pos + 1.0)[None, :] * log_gamma[:, None])
    zeta = jnp.exp((CHUNK - 1.0 - pos)[None, :] * log_gamma[:, None])
    cdec = jnp.exp(CHUNK * log_gamma)
    wide = lambda a: jnp.broadcast_to(a[:, :, None], (heads, CHUNK, HEAD_DIM))
    return (cos, sin, dmat, wide(xi), wide(zeta),
            jnp.broadcast_to(cdec[:, None, None], (heads, SUBLANES, HEAD_DIM)))


def _merge_kernel(*refs, convert_next):
    yp_ref, yf_ref, yr_ref, g0_ref, g1_ref, g2_ref, x_ref, wp_ref, wf_ref, wr_ref, wo_ref = refs[:11]
    n = 4 if convert_next else 0
    next_refs, o_ref, cast_refs = refs[11:11 + n], refs[11 + n], refs[12 + n:]
    merged = (jax.nn.sigmoid(g0_ref[...]) * _dot(yp_ref[...], wp_ref[...])
              + jax.nn.sigmoid(g1_ref[...]) * _dot(yf_ref[...], wf_ref[...])
              + jax.nn.sigmoid(g2_ref[...]) * _dot(yr_ref[...], wr_ref[...]))
    o_ref[...] = x_ref[...] + _dot(merged.astype(BF16), wo_ref[...])
    for src_ref, dst_ref in zip(next_refs, cast_refs):
        dst_ref[...] = src_ref[...].astype(BF16)


def _merge(y_pool, y_fox, y_ret, za, x, weights, *, gate_blk, next_weights=None, tm=MERGE_TM):
    t, d = x.shape
    gm = t // tm
    branch = lambda y: pl.BlockSpec((tm, y.shape[-1]), lambda m: (m, 0))
    gate = lambda i: pl.BlockSpec((tm, d), lambda m: (m, gate_blk + i))
    resident = lambda w: pl.BlockSpec(w.shape, lambda m: (0, 0), pipeline_mode=pl.Buffered(1))
    in_specs = [branch(y_pool), branch(y_fox), branch(y_ret), gate(0), gate(1), gate(2),
                pl.BlockSpec((tm, d), lambda m: (m, 0))] + [resident(w) for w in weights]
    args = [y_pool, y_fox, y_ret, za, za, za, x, *weights]
    out_specs = [pl.BlockSpec((tm, d), lambda m: (m, 0))]
    out_shape = [jax.ShapeDtypeStruct((t, d), F32)]
    if next_weights is not None:
        *stacked, nl = next_weights
        for w in stacked:
            rows = w.shape[1] // gm
            in_specs.append(pl.BlockSpec((None, rows, w.shape[2]), lambda m: (nl, m, 0)))
            out_specs.append(pl.BlockSpec((rows, w.shape[2]), lambda m: (m, 0)))
            out_shape.append(jax.ShapeDtypeStruct(w.shape[1:], BF16))
        args += stacked
    outs = pl.pallas_call(
        functools.partial(_merge_kernel, convert_next=next_weights is not None),
        grid=(gm,),
        in_specs=in_specs,
        out_specs=out_specs,
        out_shape=out_shape,
        compiler_params=_params("parallel"),
        name="merge",
    )(*args)
    return outs[0], (tuple(outs[1:]) if next_weights is not None else None)


def kernel(x, ffn1_norm, ffn1_w13, ffn1_w2, mix_norm, w_in, forget_bias, pool_w, pool_scale,
           w_branch_pool, w_branch_fox, w_branch_ret, w_out, ffn2_norm, ffn2_w13, ffn2_w2,
           final_norm):
    b, s, d = x.shape
    depth = w_in.shape[0]
    pool_width = pool_scale.shape[-1]
    fox_heads = forget_bias.shape[-1]
    fox_width = w_branch_fox.shape[1]
    ret_width = w_branch_ret.shape[1]
    ret_heads = ret_width // HEAD_DIM
    assert fox_width == fox_heads * HEAD_DIM and pool_width == fox_width == ret_width
    assert d % fox_width == 0

    o = 0
    cuts = {}
    for name, width in (("pool", pool_width), ("fq", fox_width), ("fk", fox_width),
                        ("fv", fox_width), ("fl", fox_heads), ("rq", ret_width),
                        ("rk", ret_width), ("rv", ret_width), ("rg", ret_width),
                        ("gates", N_BRANCHES * d)):
        cuts[name] = slice(o, o + width)
        o += width
    assert o == w_in.shape[-1]
    w_in_t = jnp.swapaxes(w_in, 1, 2)
    tn = pool_width
    rows_of = lambda name: list(range(cuts[name].start, cuts[name].stop, tn))
    rows_a = sum((rows_of(n) for n in ("pool", "rq", "rk", "rg", "gates")), [])
    rows_b = sum((rows_of(n) for n in ("fq", "fk", "fv", "rv")), [])
    gate_blk = (pool_width + 3 * ret_width) // d

    bias = jnp.pad(forget_bias, ((0, 0), (0, LANES - fox_heads)))[:, None, :]
    row = lambda a: a[:, None, :]
    pool_w_b = pool_w.astype(BF16)
    branch_f32 = (w_branch_pool, w_branch_fox, w_branch_ret, w_out)
    branch_w = tuple(w[0].astype(BF16) for w in branch_f32)
    tables = _retention_tables(s, ret_heads)

    halves = [(norm, w13, w2, l) for l in range(depth)
              for norm, w13, w2 in ((ffn1_norm, ffn1_w13, ffn1_w2), (ffn2_norm, ffn2_w13, ffn2_w2))]

    def half_step(xt, i, w13_b, w2_b):
        norm, _, _, l = halves[i]
        last = i == len(halves) - 1
        nxt = None if last else (halves[i + 1][1], halves[i + 1][2], halves[i + 1][3])
        return _ffn(xt, norm[l][None, :], w13_b, w2_b, next_weights=nxt,
                    final_gain=final_norm[None, :] if last else None)

    xt = x.reshape(b * s, d)
    w13_b, w2_b = ffn1_w13[0].astype(BF16), ffn1_w2[0].astype(BF16)
    for l in range(depth):
        xt, w13_b, w2_b = half_step(xt, 2 * l, w13_b, w2_b)
        gain = mix_norm[l][None, :]
        za, zb, fl = _in_proj(xt, gain, w_in_t, l, rows_a, rows_b, cuts["fl"].start, fox_heads, tn=tn)
        za3, zb3 = za.reshape(b, s, -1), zb.reshape(b, s, -1)
        cf, cft = _forget_cumsum(fl.reshape(b, s, LANES), bias, l, heads=fox_heads, tk=FOX_TQ)
        y_fox = _fox(zb3, cf, cft, width=fox_width, tq=FOX_TQ)
        y_ret, y_pool = _retention_and_pool(za3, zb3, tables, pool_w_b, row(pool_scale), l,
                                            width=pool_width, ts=RET_POOL_TS)
        flat = lambda y: y.reshape(b * s, -1)
        xt, branch_w = _merge(flat(y_pool), flat(y_fox), flat(y_ret), za, xt, branch_w,
                              gate_blk=gate_blk,
                              next_weights=(*branch_f32, l + 1) if l + 1 < depth else None)
        xt, w13_b, w2_b = half_step(xt, 2 * l + 1, w13_b, w2_b)
    return xt.reshape(b, s, d)
```

```python
import functools

import jax
import jax.numpy as jnp
from jax import lax
from jax.experimental import pallas as pl
from jax.experimental.pallas import tpu as pltpu

F32 = jnp.float32
BF16 = jnp.bfloat16

RMS_EPS = 1e-6
GN_EPS = 1e-5
ROPE_BASE = 10000.0
POOL_WINDOWS = (2, 4, 8, 16)
POOL_HALO = 16
HEAD_DIM = 128
CHUNK = 128
N_BRANCHES = 3

LANES = 128
SUBLANES = 8
VMEM_LIMIT_BYTES = 60 * 1024 * 1024

FFN_TM, FFN_TF = 1024, 512
IN_PROJ_TM = 1024
MERGE_TM = 256
FOX_TQ = 512
RET_POOL_TS = 4 * CHUNK


def _params(*semantics):
    return pltpu.CompilerParams(dimension_semantics=semantics,
                                vmem_limit_bytes=VMEM_LIMIT_BYTES)


def _rms(x, g):
    return x * lax.rsqrt(jnp.mean(x * x, axis=-1, keepdims=True) + RMS_EPS) * g


def _silu(x):
    return x * jax.nn.sigmoid(x)


def _dot(a, b):
    return jnp.dot(a, b, preferred_element_type=F32)


def _dot_nt(a, b):
    return lax.dot_general(a, b, (((1,), (1,)), ((), ())), preferred_element_type=F32)


def _ffn_kernel(*refs, final_norm, convert_next):
    refs = list(refs)
    x_ref, g_ref, w1_ref, w3_ref, w2_ref = refs[:5]
    del refs[:5]
    gf_ref = refs.pop(0) if final_norm else None
    next_w13_ref, next_w2_ref = (refs.pop(0), refs.pop(0)) if convert_next else (None, None)
    o_ref = refs.pop(0)
    cast_w13_ref, cast_w2_ref = (refs.pop(0), refs.pop(0)) if convert_next else (None, None)
    (xn_ref,) = refs
    f = pl.program_id(1)

    @pl.when(f == 0)
    def _():
        x = x_ref[...]
        xn_ref[...] = _rms(x, g_ref[...]).astype(BF16)
        o_ref[...] = x

    xn = xn_ref[...]
    h = (_silu(_dot(xn, w1_ref[...])) * _dot(xn, w3_ref[...])).astype(BF16)
    o_ref[...] += 0.5 * _dot(h, w2_ref[...])

    if final_norm:
        @pl.when(f == pl.num_programs(1) - 1)
        def _():
            o_ref[...] = _rms(o_ref[...], gf_ref[...])

    if convert_next:
        cast_w13_ref[...] = next_w13_ref[...].astype(BF16)
        cast_w2_ref[...] = next_w2_ref[...].astype(BF16)


def _ffn(x, gain, w13, w2, *, final_gain=None, next_weights=None, tm=FFN_TM, tf=FFN_TF):
    t, d = x.shape
    d_ff = w2.shape[0]
    gm, nf = t // tm, d_ff // tf
    in_specs = [
        pl.BlockSpec((tm, d), lambda m, f: (m, 0)),
        pl.BlockSpec((1, d), lambda m, f: (0, 0)),
        pl.BlockSpec((d, tf), lambda m, f: (0, f)),
        pl.BlockSpec((d, tf), lambda m, f: (0, f + nf)),
        pl.BlockSpec((tf, d), lambda m, f: (f, 0)),
    ]
    args = [x, gain, w13, w13, w2]
    out_specs = [pl.BlockSpec((tm, d), lambda m, f: (m, 0))]
    out_shape = [jax.ShapeDtypeStruct((t, d), F32)]
    if final_gain is not None:
        in_specs.append(pl.BlockSpec((1, d), lambda m, f: (0, 0)))
        args.append(final_gain)
    if next_weights is not None:
        n13, n2, nl = next_weights
        b13 = (d // gm, 2 * d_ff // nf)
        b2 = (d_ff // nf, d // gm)
        in_specs += [pl.BlockSpec((None,) + b13, lambda m, f: (nl, m, f)),
                     pl.BlockSpec((None,) + b2, lambda m, f: (nl, f, m))]
        args += [n13, n2]
        out_specs += [pl.BlockSpec(b13, lambda m, f: (m, f)), pl.BlockSpec(b2, lambda m, f: (f, m))]
        out_shape += [jax.ShapeDtypeStruct(n13.shape[1:], BF16),
                      jax.ShapeDtypeStruct(n2.shape[1:], BF16)]
    outs = pl.pallas_call(
        functools.partial(_ffn_kernel, final_norm=final_gain is not None,
                          convert_next=next_weights is not None),
        grid=(gm, nf),
        in_specs=in_specs,
        out_specs=out_specs,
        out_shape=out_shape,
        scratch_shapes=[pltpu.VMEM((tm, d), BF16)],
        compiler_params=_params("parallel", "arbitrary"),
        name="ffn",
    )(*args)
    return outs if next_weights is not None else (outs[0], None, None)


def _in_proj_kernel(rows_ref, x_ref, g_ref, w_ref, wg_ref, za_ref, zb_ref, fl_ref, xn_ref, *,
                    n_f32, gate_rows):
    del rows_ref
    j = pl.program_id(1)

    @pl.when(j == 0)
    def _():
        xn_ref[...] = _rms(x_ref[...], g_ref[...]).astype(BF16)
        row = lax.broadcasted_iota(jnp.int32, wg_ref.shape[1:], 0)
        wg = jnp.where(row < gate_rows, wg_ref[0], 0.0).astype(BF16)
        fl_ref[...] = _dot_nt(xn_ref[...], wg)

    @pl.when(j < n_f32)
    def _():
        za_ref[...] = _dot_nt(xn_ref[...], w_ref[0].astype(BF16))

    @pl.when(j >= n_f32)
    def _():
        zb_ref[...] = _dot_nt(xn_ref[...], w_ref[0].astype(BF16)).astype(zb_ref.dtype)


def _in_proj(x, gain, w_t, layer, rows_f32, rows_bf16, gate_start, gate_rows, *, tn,
             tm=IN_PROJ_TM):
    t, d = x.shape
    n_f32, n_bf16 = len(rows_f32), len(rows_bf16)
    gm, gn = t // tm, n_f32 + n_bf16
    assert all(r % SUBLANES == 0 for r in list(rows_f32) + list(rows_bf16) + [gate_start])
    table = jnp.asarray(list(rows_f32) + list(rows_bf16), jnp.int32) // SUBLANES
    return pl.pallas_call(
        functools.partial(_in_proj_kernel, n_f32=n_f32, gate_rows=gate_rows),
        grid_spec=pltpu.PrefetchScalarGridSpec(
            num_scalar_prefetch=1, grid=(gm, gn),
            in_specs=[
                pl.BlockSpec((tm, d), lambda m, j, tab: (m, 0)),
                pl.BlockSpec((1, d), lambda m, j, tab: (0, 0)),
                pl.BlockSpec((pl.Element(1), pl.Element(tn), pl.Element(d)),
                             lambda m, j, tab: (layer, tab[j] * SUBLANES, 0)),
                pl.BlockSpec((pl.Element(1), pl.Element(LANES), pl.Element(d)),
                             lambda m, j, tab: (layer, gate_start, 0)),
            ],
            out_specs=[
                pl.BlockSpec((tm, tn), lambda m, j, tab: (m, jnp.minimum(j, n_f32 - 1))),
                pl.BlockSpec((tm, tn), lambda m, j, tab: (m, jnp.maximum(j - n_f32, 0))),
                pl.BlockSpec((tm, LANES), lambda m, j, tab: (m, 0)),
            ],
            scratch_shapes=[pltpu.VMEM((tm, d), BF16)]),
        out_shape=[jax.ShapeDtypeStruct((t, n_f32 * tn), F32),
                   jax.ShapeDtypeStruct((t, n_bf16 * tn), BF16),
                   jax.ShapeDtypeStruct((t, LANES), F32)],
        compiler_params=_params("parallel", "arbitrary"),
        name="in_proj",
    )(table, x, gain, w_t, w_t)


def _forget_cumsum_kernel(fl_ref, b_ref, cf_ref, cft_ref, *, heads, tk):
    z = fl_ref[...] + b_ref[...]
    x = jnp.minimum(z, 0.0) - jnp.log1p(jnp.exp(-jnp.abs(z)))
    s = x.shape[0]
    row = lax.broadcasted_iota(jnp.int32, x.shape, 0)
    shift = 1
    while shift < s:
        if shift < SUBLANES:
            prev = jnp.where(row >= shift, pltpu.roll(x, shift, axis=0), 0.0)
        else:
            prev = jnp.concatenate([jnp.zeros((shift, x.shape[1]), F32), x[:s - shift]], axis=0)
        x = x + prev
        shift *= 2
    cf_ref[...] = x
    xt = x.T
    for j in range(s // tk):
        cft_ref[j] = xt[:heads, j * tk:(j + 1) * tk]


def _forget_cumsum(fl, bias, layer, *, heads, tk):
    b, s, w = fl.shape
    return pl.pallas_call(
        functools.partial(_forget_cumsum_kernel, heads=heads, tk=tk),
        grid=(b,),
        in_specs=[
            pl.BlockSpec((None, s, w), lambda i: (i, 0, 0)),
            pl.BlockSpec((None, 1, w), lambda i: (layer, 0, 0)),
        ],
        out_specs=[
            pl.BlockSpec((None, s, w), lambda i: (i, 0, 0)),
            pl.BlockSpec((None, s // tk, heads, tk), lambda i: (i, 0, 0, 0)),
        ],
        out_shape=[
            jax.ShapeDtypeStruct((b, s, w), F32),
            jax.ShapeDtypeStruct((b, s // tk, heads, tk), F32),
        ],
        compiler_params=_params("parallel"),
        name="forget_cumsum",
    )(fl, bias)


def _fox_kernel(q_ref, k_ref, v_ref, cf_ref, cft_ref, o_ref, fq_ref, m_ref, accl_ref, *, tq):
    i = pl.program_id(1)
    heads = q_ref.shape[-1] // HEAD_DIM
    log2e = 1.4426950408889634
    c1 = HEAD_DIM ** -0.5 * log2e
    causal = (lax.broadcasted_iota(jnp.int32, (tq, tq), 0)
              >= lax.broadcasted_iota(jnp.int32, (tq, tq), 1))
    ones = jnp.ones((tq, HEAD_DIM), BF16)
    wide = lambda a, lanes: jnp.concatenate([a] * (lanes // LANES), axis=1)

    cf = cf_ref[...] * log2e
    for h in range(heads):
        fq_ref[h] = jnp.broadcast_to(cf[:, h:h + 1], (tq, LANES))
    m_ref[...] = jnp.full(m_ref.shape, -jnp.inf, F32)
    accl_ref[...] = jnp.zeros(accl_ref.shape, F32)

    def block(j, masked):
        k0 = pl.multiple_of(j * tq, tq)
        for h in range(heads):
            lanes = slice(h * HEAD_DIM, (h + 1) * HEAD_DIM)
            kb = k_ref[pl.ds(k0, tq), lanes]
            v1 = jnp.concatenate([v_ref[pl.ds(k0, tq), lanes], ones], axis=1)
            fk = cft_ref[j, h:h + 1, :] * log2e
            t = _dot_nt(q_ref[:, lanes], kb) * c1 + wide(fq_ref[h], tq) - fk
            if masked:
                t = jnp.where(causal, t, -jnp.inf)
            m_prev = m_ref[h]
            m_new = jnp.maximum(m_prev, jnp.max(t, axis=-1, keepdims=True))
            p = jnp.exp2(t - wide(m_new, tq))
            alpha = wide(jnp.exp2(m_prev - m_new), 2 * HEAD_DIM)
            accl_ref[h] = alpha * accl_ref[h] + _dot(p.astype(BF16), v1)
            m_ref[h] = m_new

    lax.fori_loop(0, i, lambda j, c: (block(j, False), c)[1], 0)
    block(i, True)
    for h in range(heads):
        accl = accl_ref[h]
        o_ref[:, h * HEAD_DIM:(h + 1) * HEAD_DIM] = (
            accl[:, :HEAD_DIM] / accl[:, HEAD_DIM:]).astype(o_ref.dtype)


def _fox(zb, cf, cft, *, width, tq):
    b, s, _ = zb.shape
    heads = width // HEAD_DIM
    return pl.pallas_call(
        functools.partial(_fox_kernel, tq=tq),
        grid=(b, s // tq),
        in_specs=[
            pl.BlockSpec((None, tq, width), lambda bi, i: (bi, i, 0)),
            pl.BlockSpec((None, s, width), lambda bi, i: (bi, 0, 1)),
            pl.BlockSpec((None, s, width), lambda bi, i: (bi, 0, 2)),
            pl.BlockSpec((None, tq, LANES), lambda bi, i: (bi, i, 0)),
            pl.BlockSpec((None, s // tq, heads, tq), lambda bi, i: (bi, 0, 0, 0)),
        ],
        out_specs=pl.BlockSpec((None, tq, width), lambda bi, i: (bi, i, 0)),
        out_shape=jax.ShapeDtypeStruct((b, s, width), BF16),
        scratch_shapes=[pltpu.VMEM((heads, tq, LANES), F32),
                        pltpu.VMEM((heads, tq, LANES), F32),
                        pltpu.VMEM((heads, tq, 2 * HEAD_DIM), F32)],
        compiler_params=_params("parallel", "arbitrary"),
        name="fox",
    )(zb, zb, zb, cf, cft)


def _retention_tile(q_ref, k_ref, v_ref, gate_ref, cos_ref, sin_ref, dm_ref, xi_ref, zeta_ref,
                    cd_ref, o_ref, state_ref):
    heads = state_ref.shape[0]
    for c in range(q_ref.shape[0] // CHUNK):
        r = slice(c * CHUNK, (c + 1) * CHUNK)
        cos, sin = cos_ref[r, :], sin_ref[r, :]
        for h in range(heads):
            lanes = slice(h * HEAD_DIM, (h + 1) * HEAD_DIM)
            q, k = q_ref[r, lanes], k_ref[r, lanes]
            q = q * cos + pltpu.roll(q, HEAD_DIM // 2, axis=1) * sin
            k = (k * cos + pltpu.roll(k, HEAD_DIM // 2, axis=1) * sin) * (HEAD_DIM ** -0.5)
            v = v_ref[r, lanes]
            state = state_ref[h]
            scores = _dot_nt(q.astype(BF16), k.astype(BF16)) * dm_ref[h]
            o = _dot(scores.astype(BF16), v)
            o = o + _dot((q * xi_ref[h]).astype(BF16), state.astype(BF16))
            kz_t = (k * zeta_ref[h]).T.astype(BF16)
            state_ref[h] = cd_ref[h, 0:1, :] * state + _dot(kz_t, v)
            mu = jnp.mean(o, axis=-1, keepdims=True)
            var = jnp.mean(jnp.square(o - mu), axis=-1, keepdims=True)
            o = (o - mu) * lax.rsqrt(var + GN_EPS)
            o_ref[r, lanes] = (_silu(gate_ref[r, lanes]) * o).astype(o_ref.dtype)


def _pool_tile(u_ref, w_ref, sc_ref, o_ref, ext_ref, seq_tile):
    ts = u_ref.shape[0]
    gd = u_ref.shape[-1] // len(POOL_WINDOWS)
    ext_ref[POOL_HALO:POOL_HALO + ts, :] = u_ref[...]
    t = seq_tile * ts + lax.broadcasted_iota(jnp.int32, (ts, 1), 0)
    for g, w in enumerate(POOL_WINDOWS):
        cols = slice(g * gd, (g + 1) * gd)
        u = ext_ref[POOL_HALO:POOL_HALO + ts, cols]
        win = u
        for k in range(1, w):
            win = win + ext_ref[POOL_HALO - k:POOL_HALO - k + ts, cols]
        cnt = jnp.minimum(t + 1, w).astype(F32)
        pooled = win / cnt - u
        mixed = _dot(pooled.astype(BF16), w_ref[g])
        o_ref[:, cols] = (mixed * sc_ref[:, cols]).astype(o_ref.dtype)


def _ret_pool_kernel(u_ref, q_ref, k_ref, gate_ref, v_ref, cos_ref, sin_ref, dm_ref, xi_ref, zeta_ref,
                     cd_ref, pw_ref, psc_ref, yr_ref, yp_ref, state_ref, ext_ref):
    i = pl.program_id(1)
    ts = u_ref.shape[0]

    @pl.when(i == 0)
    def _():
        state_ref[...] = jnp.zeros_like(state_ref)
        ext_ref[0:POOL_HALO, :] = jnp.zeros((POOL_HALO, ext_ref.shape[1]), F32)

    @pl.when(i > 0)
    def _():
        ext_ref[0:POOL_HALO, :] = ext_ref[ts:ts + POOL_HALO, :]

    _pool_tile(u_ref, pw_ref, psc_ref, yp_ref, ext_ref, i)
    _retention_tile(q_ref, k_ref, v_ref, gate_ref, cos_ref, sin_ref, dm_ref, xi_ref, zeta_ref,
                    cd_ref, yr_ref, state_ref)


def _retention_and_pool(za, zb, tables, pool_w, pool_scale, layer, *, width, ts):
    b, s, _ = za.shape
    heads = width // HEAD_DIM
    cos, sin, dmat, xi, zeta, cdec = tables
    tok = lambda blk: pl.BlockSpec((None, ts, width), lambda bi, i: (bi, i, blk))
    whole = lambda a: pl.BlockSpec(a.shape, lambda bi, i: (0, 0, 0))
    pos = pl.BlockSpec((ts, HEAD_DIM), lambda bi, i: (i, 0))
    groups, gd = pool_w.shape[1], pool_w.shape[2]
    out = jax.ShapeDtypeStruct((b, s, width), BF16)
    return pl.pallas_call(
        _ret_pool_kernel,
        grid=(b, s // ts),
        in_specs=[tok(0), tok(1), tok(2), tok(3), tok(3), pos, pos,
                  whole(dmat), whole(xi), whole(zeta), whole(cdec),
                  pl.BlockSpec((None, groups, gd, gd), lambda bi, i: (layer, 0, 0, 0)),
                  pl.BlockSpec((None, 1, width), lambda bi, i: (layer, 0, 0))],
        out_specs=[tok(0), tok(0)],
        out_shape=[out, out],
        scratch_shapes=[pltpu.VMEM((heads, HEAD_DIM, HEAD_DIM), F32),
                        pltpu.VMEM((POOL_HALO + ts, width), F32)],
        compiler_params=_params("parallel", "arbitrary"),
        name="retention_pool",
    )(za, za, za, za, zb, cos, sin, dmat, xi, zeta, cdec, pool_w, pool_scale)


def _retention_tables(s, heads):
    half = HEAD_DIM // 2
    inv_freq = ROPE_BASE ** (-jnp.arange(half, dtype=F32) / half)
    ang = jnp.arange(s, dtype=F32)[:, None] * inv_freq[None, :]
    cos = jnp.concatenate([jnp.cos(ang), jnp.cos(ang)], axis=-1)
    sin = jnp.concatenate([-jnp.sin(ang), jnp.sin(ang)], axis=-1)
    log_gamma = jnp.log1p(-jnp.power(2.0, -5.0 - jnp.arange(heads, dtype=F32)))
    pos = jnp.arange(CHUNK, dtype=F32)
    diff = pos[:, None] - pos[None, :]
    dmat = jnp.where(diff[None] >= 0,
                     jnp.exp(jnp.maximum(diff, 0.0)[None] * log_gamma[:, None, None]), 0.0)
    xi = jnp.exp((pos + 1.0)[None, :] * log_gamma[:, None])
    zeta = jnp.exp((CHUNK - 1.0 - pos)[None, :] * log_gamma[:, None])
    cdec = jnp.exp(CHUNK * log_gamma)
    wide = lambda a: jnp.broadcast_to(a[:, :, None], (heads, CHUNK, HEAD_DIM))
    return (cos, sin, dmat, wide(xi), wide(zeta),
            jnp.broadcast_to(cdec[:, None, None], (heads, SUBLANES, HEAD_DIM)))


def _merge_kernel(*refs, convert_next):
    yp_ref, yf_ref, yr_ref, g0_ref, g1_ref, g2_ref, x_ref, wp_ref, wf_ref, wr_ref, wo_ref = refs[:11]
    n = 4 if convert_next else 0
    next_refs, o_ref, cast_refs = refs[11:11 + n], refs[11 + n], refs[12 + n:]
    gate = lambda g_ref: jax.nn.sigmoid(g_ref[...].astype(F32))
    merged = (gate(g0_ref) * _dot(yp_ref[...], wp_ref[...])
              + gate(g1_ref) * _dot(yf_ref[...], wf_ref[...])
              + gate(g2_ref) * _dot(yr_ref[...], wr_ref[...]))
    o_ref[...] = x_ref[...] + _dot(merged.astype(BF16), wo_ref[...])
    for src_ref, dst_ref in zip(next_refs, cast_refs):
        dst_ref[...] = src_ref[...].astype(BF16)


def _merge(y_pool, y_fox, y_ret, zg, x, weights, *, gate_blk, next_weights=None, tm=MERGE_TM):
    t, d = x.shape
    gm = t // tm
    branch = lambda y: pl.BlockSpec((tm, y.shape[-1]), lambda m: (m, 0))
    gate = lambda i: pl.BlockSpec((tm, d), lambda m: (m, gate_blk + i))
    resident = lambda w: pl.BlockSpec(w.shape, lambda m: (0, 0), pipeline_mode=pl.Buffered(1))
    in_specs = [branch(y_pool), branch(y_fox), branch(y_ret), gate(0), gate(1), gate(2),
                pl.BlockSpec((tm, d), lambda m: (m, 0))] + [resident(w) for w in weights]
    args = [y_pool, y_fox, y_ret, zg, zg, zg, x, *weights]
    out_specs = [pl.BlockSpec((tm, d), lambda m: (m, 0))]
    out_shape = [jax.ShapeDtypeStruct((t, d), F32)]
    if next_weights is not None:
        *stacked, nl = next_weights
        for w in stacked:
            rows = w.shape[1] // gm
            in_specs.append(pl.BlockSpec((None, rows, w.shape[2]), lambda m: (nl, m, 0)))
            out_specs.append(pl.BlockSpec((rows, w.shape[2]), lambda m: (m, 0)))
            out_shape.append(jax.ShapeDtypeStruct(w.shape[1:], BF16))
        args += stacked
    outs = pl.pallas_call(
        functools.partial(_merge_kernel, convert_next=next_weights is not None),
        grid=(gm,),
        in_specs=in_specs,
        out_specs=out_specs,
        out_shape=out_shape,
        compiler_params=_params("parallel"),
        name="merge",
    )(*args)
    return outs[0], (tuple(outs[1:]) if next_weights is not None else None)


def kernel(x, ffn1_norm, ffn1_w13, ffn1_w2, mix_norm, w_in, forget_bias, pool_w, pool_scale,
           w_branch_pool, w_branch_fox, w_branch_ret, w_out, ffn2_norm, ffn2_w13, ffn2_w2,
           final_norm):
    b, s, d = x.shape
    depth = w_in.shape[0]
    pool_width = pool_scale.shape[-1]
    fox_heads = forget_bias.shape[-1]
    fox_width = w_branch_fox.shape[1]
    ret_width = w_branch_ret.shape[1]
    ret_heads = ret_width // HEAD_DIM
    assert fox_width == fox_heads * HEAD_DIM and pool_width == fox_width == ret_width
    assert d % fox_width == 0

    o = 0
    cuts = {}
    for name, width in (("pool", pool_width), ("fq", fox_width), ("fk", fox_width),
                        ("fv", fox_width), ("fl", fox_heads), ("rq", ret_width),
                        ("rk", ret_width), ("rv", ret_width), ("rg", ret_width),
                        ("gates", N_BRANCHES * d)):
        cuts[name] = slice(o, o + width)
        o += width
    assert o == w_in.shape[-1]
    w_in_t = jnp.swapaxes(w_in, 1, 2)
    tn = pool_width
    rows_of = lambda name: list(range(cuts[name].start, cuts[name].stop, tn))
    rows_a = sum((rows_of(n) for n in ("pool", "rq", "rk", "rg")), [])
    rows_b = sum((rows_of(n) for n in ("fq", "fk", "fv", "rv", "gates")), [])
    gate_blk = (3 * fox_width + ret_width) // d

    bias = jnp.pad(forget_bias, ((0, 0), (0, LANES - fox_heads)))[:, None, :]
    row = lambda a: a[:, None, :]
    pool_w_b = pool_w.astype(BF16)
    branch_f32 = (w_branch_pool, w_branch_fox, w_branch_ret, w_out)
    branch_w = tuple(w[0].astype(BF16) for w in branch_f32)
    tables = _retention_tables(s, ret_heads)

    halves = [(norm, w13, w2, l) for l in range(depth)
              for norm, w13, w2 in ((ffn1_norm, ffn1_w13, ffn1_w2), (ffn2_norm, ffn2_w13, ffn2_w2))]

    def half_step(xt, i, w13_b, w2_b):
        norm, _, _, l = halves[i]
        last = i == len(halves) - 1
        nxt = None if last else (halves[i + 1][1], halves[i + 1][2], halves[i + 1][3])
        return _ffn(xt, norm[l][None, :], w13_b, w2_b, next_weights=nxt,
                    final_gain=final_norm[None, :] if last else None)

    xt = x.reshape(b * s, d)
    w13_b, w2_b = ffn1_w13[0].astype(BF16), ffn1_w2[0].astype(BF16)
    for l in range(depth):
        xt, w13_b, w2_b = half_step(xt, 2 * l, w13_b, w2_b)
        gain = mix_norm[l][None, :]
        za, zb, fl = _in_proj(xt, gain, w_in_t, l, rows_a, rows_b, cuts["fl"].start, fox_heads, tn=tn)
        za3, zb3 = za.reshape(b, s, -1), zb.reshape(b, s, -1)
        cf, cft = _forget_cumsum(fl.reshape(b, s, LANES), bias, l, heads=fox_heads, tk=FOX_TQ)
        y_fox = _fox(zb3, cf, cft, width=fox_width, tq=FOX_TQ)
        y_ret, y_pool = _retention_and_pool(za3, zb3, tables, pool_w_b, row(pool_scale), l,
                                            width=pool_width, ts=RET_POOL_TS)
        flat = lambda y: y.reshape(b * s, -1)
        xt, branch_w = _merge(flat(y_pool), flat(y_fox), flat(y_ret), zb, xt, branch_w,
                              gate_blk=gate_blk,
                              next_weights=(*branch_f32, l + 1) if l + 1 < depth else None)
        xt, w13_b, w2_b = half_step(xt, 2 * l + 1, w13_b, w2_b)
    return xt.reshape(b, s, d)
```

```python
import functools

import jax
import jax.numpy as jnp
from jax import lax
from jax.experimental import pallas as pl
from jax.experimental.pallas import tpu as pltpu

F32 = jnp.float32
BF16 = jnp.bfloat16

RMS_EPS = 1e-6
GN_EPS = 1e-5
ROPE_BASE = 10000.0
POOL_WINDOWS = (2, 4, 8, 16)
POOL_HALO = 16
HEAD_DIM = 128
CHUNK = 128
N_BRANCHES = 3

LANES = 128
SUBLANES = 8
VMEM_LIMIT_BYTES = 60 * 1024 * 1024

FFN_TM, FFN_TF = 1024, 512
IN_PROJ_TM = 1024
MERGE_TM = 256
FOX_TQ = 512
RET_POOL_TS = 4 * CHUNK


def _params(*semantics):
    return pltpu.CompilerParams(dimension_semantics=semantics,
                                vmem_limit_bytes=VMEM_LIMIT_BYTES)


def _rms(x, g):
    return x * lax.rsqrt(jnp.mean(x * x, axis=-1, keepdims=True) + RMS_EPS) * g


def _silu(x):
    return x * jax.nn.sigmoid(x)


def _dot(a, b):
    return jnp.dot(a, b, preferred_element_type=F32)


def _dot_nt(a, b):
    return lax.dot_general(a, b, (((1,), (1,)), ((), ())), preferred_element_type=F32)


def _ffn_kernel(*refs, final_norm, convert_next, next_layer, tf, gm):
    refs = list(refs)
    x_ref, g_ref, w13_hbm, w2_hbm = refs[:4]
    del refs[:4]
    gf_ref = refs.pop(0) if final_norm else None
    n13_hbm, n2_hbm = (refs.pop(0), refs.pop(0)) if convert_next else (None, None)
    o_ref = refs.pop(0)
    c13_hbm, c2_hbm = (refs.pop(0), refs.pop(0)) if convert_next else (None, None)
    xn_ref, w1_buf, w3_buf, w2_buf = refs[:4]
    del refs[:4]
    if convert_next:
        n13_buf, n2_buf, c13_buf, c2_buf, in_sem, out_sem = refs
    else:
        (in_sem,) = refs
    m = pl.program_id(0)
    d_ff = w2_hbm.shape[0]
    nf = d_ff // tf
    chunk_slot = lambda c: (m * nf + c) % 2

    def copies_in(tile, c, slot):
        cps = [pltpu.make_async_copy(w13_hbm.at[:, pl.ds(c * tf, tf)], w1_buf.at[slot], in_sem.at[0, slot]),
               pltpu.make_async_copy(w13_hbm.at[:, pl.ds(d_ff + c * tf, tf)], w3_buf.at[slot],
                                     in_sem.at[1, slot]),
               pltpu.make_async_copy(w2_hbm.at[pl.ds(c * tf, tf), :], w2_buf.at[slot], in_sem.at[2, slot])]
        if convert_next:
            r13, w13c = n13_buf.shape[1:]
            r2, w2c = n2_buf.shape[1:]
            cps += [pltpu.make_async_copy(
                        n13_hbm.at[next_layer, pl.ds(pl.multiple_of(tile * r13, r13), r13),
                                   pl.ds(c * w13c, w13c)],
                        n13_buf.at[slot], in_sem.at[3, slot]),
                    pltpu.make_async_copy(
                        n2_hbm.at[next_layer, pl.ds(c * r2, r2),
                                  pl.ds(pl.multiple_of(tile * w2c, w2c), w2c)],
                        n2_buf.at[slot], in_sem.at[4, slot])]
        return cps

    def copies_out(tile, c, slot):
        r13, w13c = c13_buf.shape[1:]
        r2, w2c = c2_buf.shape[1:]
        return [pltpu.make_async_copy(
                    c13_buf.at[slot],
                    c13_hbm.at[pl.ds(pl.multiple_of(tile * r13, r13), r13), pl.ds(c * w13c, w13c)],
                    out_sem.at[0, slot]),
                pltpu.make_async_copy(
                    c2_buf.at[slot],
                    c2_hbm.at[pl.ds(c * r2, r2), pl.ds(pl.multiple_of(tile * w2c, w2c), w2c)],
                    out_sem.at[1, slot])]

    @pl.when(m == 0)
    def _():
        for cp in copies_in(0, 0, 0):
            cp.start()

    x = x_ref[...]
    xn_ref[...] = _rms(x, g_ref[...]).astype(BF16)
    o_ref[...] = x

    for c in range(nf):
        slot = chunk_slot(c)
        if c + 1 < nf:
            for cp in copies_in(m, c + 1, 1 - slot):
                cp.start()
        else:
            @pl.when(m + 1 < gm)
            def _():
                for cp in copies_in(m + 1, 0, 1 - slot):
                    cp.start()
        for cp in copies_in(m, c, slot):
            cp.wait()
        xn = xn_ref[...]
        h = (_silu(_dot(xn, w1_buf[slot])) * _dot(xn, w3_buf[slot])).astype(BF16)
        o_ref[...] += 0.5 * _dot(h, w2_buf[slot])
        if convert_next:
            if c >= 2:
                for cp in copies_out(m, c - 2, slot):
                    cp.wait()
            else:
                @pl.when(m > 0)
                def _():
                    for cp in copies_out(m - 1, nf - 2 + c, slot):
                        cp.wait()
            c13_buf[slot] = n13_buf[slot].astype(BF16)
            c2_buf[slot] = n2_buf[slot].astype(BF16)
            for cp in copies_out(m, c, slot):
                cp.start()

    if convert_next:
        @pl.when(m == gm - 1)
        def _():
            for c in (nf - 2, nf - 1):
                for cp in copies_out(m, c, chunk_slot(c)):
                    cp.wait()

    if final_norm:
        o_ref[...] = _rms(o_ref[...], gf_ref[...])


def _ffn(x, gain, w13, w2, *, final_gain=None, next_weights=None, tm=FFN_TM, tf=FFN_TF):
    t, d = x.shape
    d_ff = w2.shape[0]
    gm, nf = t // tm, d_ff // tf
    anywhere = pl.BlockSpec(memory_space=pl.ANY)
    in_specs = [pl.BlockSpec((tm, d), lambda m: (m, 0)), pl.BlockSpec((1, d), lambda m: (0, 0)),
                anywhere, anywhere]
    args = [x, gain, w13, w2]
    out_specs = [pl.BlockSpec((tm, d), lambda m: (m, 0))]
    out_shape = [jax.ShapeDtypeStruct((t, d), F32)]
    scratch = [pltpu.VMEM((tm, d), BF16), pltpu.VMEM((2, d, tf), BF16), pltpu.VMEM((2, d, tf), BF16),
               pltpu.VMEM((2, tf, d), BF16)]
    if final_gain is not None:
        in_specs.append(pl.BlockSpec((1, d), lambda m: (0, 0)))
        args.append(final_gain)
    if next_weights is not None:
        n13, n2, nl = next_weights
        b13 = (d // gm, 2 * d_ff // nf)
        b2 = (d_ff // nf, d // gm)
        in_specs += [anywhere, anywhere]
        args += [n13, n2]
        out_specs += [anywhere, anywhere]
        out_shape += [jax.ShapeDtypeStruct(n13.shape[1:], BF16), jax.ShapeDtypeStruct(n2.shape[1:], BF16)]
        scratch += [pltpu.VMEM((2,) + b13, F32), pltpu.VMEM((2,) + b2, F32),
                    pltpu.VMEM((2,) + b13, BF16), pltpu.VMEM((2,) + b2, BF16),
                    pltpu.SemaphoreType.DMA((5, 2)), pltpu.SemaphoreType.DMA((2, 2))]
    else:
        scratch.append(pltpu.SemaphoreType.DMA((3, 2)))
    outs = pl.pallas_call(
        functools.partial(_ffn_kernel, final_norm=final_gain is not None,
                          convert_next=next_weights is not None,
                          next_layer=next_weights[2] if next_weights is not None else None, tf=tf, gm=gm),
        grid=(gm,),
        in_specs=in_specs,
        out_specs=out_specs,
        out_shape=out_shape,
        scratch_shapes=scratch,
        compiler_params=_params("arbitrary"),
        name="ffn",
    )(*args)
    return outs if next_weights is not None else (outs[0], None, None)


def _in_proj_kernel(rows_ref, x_ref, g_ref, w_ref, wg_ref, za_ref, zb_ref, fl_ref, xn_ref, *,
                    n_f32, gate_rows):
    del rows_ref
    j = pl.program_id(1)

    @pl.when(j == 0)
    def _():
        xn_ref[...] = _rms(x_ref[...], g_ref[...]).astype(BF16)
        row = lax.broadcasted_iota(jnp.int32, wg_ref.shape[1:], 0)
        wg = jnp.where(row < gate_rows, wg_ref[0], 0.0).astype(BF16)
        fl_ref[...] = _dot_nt(xn_ref[...], wg)

    @pl.when(j < n_f32)
    def _():
        za_ref[...] = _dot_nt(xn_ref[...], w_ref[0].astype(BF16))

    @pl.when(j >= n_f32)
    def _():
        zb_ref[...] = _dot_nt(xn_ref[...], w_ref[0].astype(BF16)).astype(zb_ref.dtype)


def _in_proj(x, gain, w_t, layer, rows_f32, rows_bf16, gate_start, gate_rows, *, tn,
             tm=IN_PROJ_TM):
    t, d = x.shape
    n_f32, n_bf16 = len(rows_f32), len(rows_bf16)
    gm, gn = t // tm, n_f32 + n_bf16
    assert all(r % SUBLANES == 0 for r in list(rows_f32) + list(rows_bf16) + [gate_start])
    table = jnp.asarray(list(rows_f32) + list(rows_bf16), jnp.int32) // SUBLANES
    return pl.pallas_call(
        functools.partial(_in_proj_kernel, n_f32=n_f32, gate_rows=gate_rows),
        grid_spec=pltpu.PrefetchScalarGridSpec(
            num_scalar_prefetch=1, grid=(gm, gn),
            in_specs=[
                pl.BlockSpec((tm, d), lambda m, j, tab: (m, 0)),
                pl.BlockSpec((1, d), lambda m, j, tab: (0, 0)),
                pl.BlockSpec((pl.Element(1), pl.Element(tn), pl.Element(d)),
                             lambda m, j, tab: (layer, tab[j] * SUBLANES, 0)),
                pl.BlockSpec((pl.Element(1), pl.Element(LANES), pl.Element(d)),
                             lambda m, j, tab: (layer, gate_start, 0)),
            ],
            out_specs=[
                pl.BlockSpec((tm, tn), lambda m, j, tab: (m, jnp.minimum(j, n_f32 - 1))),
                pl.BlockSpec((tm, tn), lambda m, j, tab: (m, jnp.maximum(j - n_f32, 0))),
                pl.BlockSpec((tm, LANES), lambda m, j, tab: (m, 0)),
            ],
            scratch_shapes=[pltpu.VMEM((tm, d), BF16)]),
        out_shape=[jax.ShapeDtypeStruct((t, n_f32 * tn), F32),
                   jax.ShapeDtypeStruct((t, n_bf16 * tn), BF16),
                   jax.ShapeDtypeStruct((t, LANES), F32)],
        compiler_params=_params("parallel", "arbitrary"),
        name="in_proj",
    )(table, x, gain, w_t, w_t)


def _forget_cumsum_kernel(fl_ref, b_ref, cf_ref, cft_ref, *, heads, tk):
    z = fl_ref[...] + b_ref[...]
    x = jnp.minimum(z, 0.0) - jnp.log1p(jnp.exp(-jnp.abs(z)))
    s = x.shape[0]
    row = lax.broadcasted_iota(jnp.int32, x.shape, 0)
    shift = 1
    while shift < s:
        if shift < SUBLANES:
            prev = jnp.where(row >= shift, pltpu.roll(x, shift, axis=0), 0.0)
        else:
            prev = jnp.concatenate([jnp.zeros((shift, x.shape[1]), F32), x[:s - shift]], axis=0)
        x = x + prev
        shift *= 2
    cf_ref[...] = x
    xt = x.T
    for j in range(s // tk):
        cft_ref[j] = xt[:heads, j * tk:(j + 1) * tk]


def _forget_cumsum(fl, bias, layer, *, heads, tk):
    b, s, w = fl.shape
    return pl.pallas_call(
        functools.partial(_forget_cumsum_kernel, heads=heads, tk=tk),
        grid=(b,),
        in_specs=[
            pl.BlockSpec((None, s, w), lambda i: (i, 0, 0)),
            pl.BlockSpec((None, 1, w), lambda i: (layer, 0, 0)),
        ],
        out_specs=[
            pl.BlockSpec((None, s, w), lambda i: (i, 0, 0)),
            pl.BlockSpec((None, s // tk, heads, tk), lambda i: (i, 0, 0, 0)),
        ],
        out_shape=[
            jax.ShapeDtypeStruct((b, s, w), F32),
            jax.ShapeDtypeStruct((b, s // tk, heads, tk), F32),
        ],
        compiler_params=_params("parallel"),
        name="forget_cumsum",
    )(fl, bias)


def _fox_kernel(q_ref, k_ref, v_ref, cf_ref, cft_ref, o_ref, fq_ref, m_ref, accl_ref, *, tq):
    i = pl.program_id(1)
    heads = q_ref.shape[-1] // HEAD_DIM
    log2e = 1.4426950408889634
    c1 = HEAD_DIM ** -0.5 * log2e
    causal = (lax.broadcasted_iota(jnp.int32, (tq, tq), 0)
              >= lax.broadcasted_iota(jnp.int32, (tq, tq), 1))
    ones = jnp.ones((tq, HEAD_DIM), BF16)
    wide = lambda a, lanes: jnp.concatenate([a] * (lanes // LANES), axis=1)

    cf = cf_ref[...] * log2e
    for h in range(heads):
        fq_ref[h] = jnp.broadcast_to(cf[:, h:h + 1], (tq, LANES))
    m_ref[...] = jnp.full(m_ref.shape, -jnp.inf, F32)
    accl_ref[...] = jnp.zeros(accl_ref.shape, F32)

    def block(j, masked):
        k0 = pl.multiple_of(j * tq, tq)
        for h in range(heads):
            lanes = slice(h * HEAD_DIM, (h + 1) * HEAD_DIM)
            kb = k_ref[pl.ds(k0, tq), lanes]
            v1 = jnp.concatenate([v_ref[pl.ds(k0, tq), lanes], ones], axis=1)
            fk = cft_ref[j, h:h + 1, :] * log2e
            t = _dot_nt(q_ref[:, lanes], kb) * c1 + wide(fq_ref[h], tq) - fk
            if masked:
                t = jnp.where(causal, t, -jnp.inf)
            m_prev = m_ref[h]
            m_new = jnp.maximum(m_prev, jnp.max(t, axis=-1, keepdims=True))
            p = jnp.exp2(t - wide(m_new, tq))
            alpha = wide(jnp.exp2(m_prev - m_new), 2 * HEAD_DIM)
            accl_ref[h] = alpha * accl_ref[h] + _dot(p.astype(BF16), v1)
            m_ref[h] = m_new

    lax.fori_loop(0, i, lambda j, c: (block(j, False), c)[1], 0)
    block(i, True)
    for h in range(heads):
        accl = accl_ref[h]
        o_ref[:, h * HEAD_DIM:(h + 1) * HEAD_DIM] = (
            accl[:, :HEAD_DIM] / accl[:, HEAD_DIM:]).astype(o_ref.dtype)


def _fox(zb, cf, cft, *, width, tq):
    b, s, _ = zb.shape
    heads = width // HEAD_DIM
    return pl.pallas_call(
        functools.partial(_fox_kernel, tq=tq),
        grid=(b, s // tq),
        in_specs=[
            pl.BlockSpec((None, tq, width), lambda bi, i: (bi, i, 0)),
            pl.BlockSpec((None, s, width), lambda bi, i: (bi, 0, 1)),
            pl.BlockSpec((None, s, width), lambda bi, i: (bi, 0, 2)),
            pl.BlockSpec((None, tq, LANES), lambda bi, i: (bi, i, 0)),
            pl.BlockSpec((None, s // tq, heads, tq), lambda bi, i: (bi, 0, 0, 0)),
        ],
        out_specs=pl.BlockSpec((None, tq, width), lambda bi, i: (bi, i, 0)),
        out_shape=jax.ShapeDtypeStruct((b, s, width), BF16),
        scratch_shapes=[pltpu.VMEM((heads, tq, LANES), F32),
                        pltpu.VMEM((heads, tq, LANES), F32),
                        pltpu.VMEM((heads, tq, 2 * HEAD_DIM), F32)],
        compiler_params=_params("parallel", "arbitrary"),
        name="fox",
    )(zb, zb, zb, cf, cft)


def _retention_tile(q_ref, k_ref, v_ref, gate_ref, cos_ref, sin_ref, dm_ref, xi_ref, zeta_ref,
                    cd_ref, o_ref, state_ref):
    heads = state_ref.shape[0]
    for c in range(q_ref.shape[0] // CHUNK):
        r = slice(c * CHUNK, (c + 1) * CHUNK)
        cos, sin = cos_ref[r, :], sin_ref[r, :]
        for h in range(heads):
            lanes = slice(h * HEAD_DIM, (h + 1) * HEAD_DIM)
            q, k = q_ref[r, lanes], k_ref[r, lanes]
            q = q * cos + pltpu.roll(q, HEAD_DIM // 2, axis=1) * sin
            k = (k * cos + pltpu.roll(k, HEAD_DIM // 2, axis=1) * sin) * (HEAD_DIM ** -0.5)
            v = v_ref[r, lanes]
            state = state_ref[h]
            scores = _dot_nt(q.astype(BF16), k.astype(BF16)) * dm_ref[h]
            o = _dot(scores.astype(BF16), v)
            o = o + _dot((q * xi_ref[h]).astype(BF16), state.astype(BF16))
            kz_t = (k * zeta_ref[h]).T.astype(BF16)
            state_ref[h] = cd_ref[h, 0:1, :] * state + _dot(kz_t, v)
            mu = jnp.mean(o, axis=-1, keepdims=True)
            var = jnp.mean(jnp.square(o - mu), axis=-1, keepdims=True)
            o = (o - mu) * lax.rsqrt(var + GN_EPS)
            o_ref[r, lanes] = (_silu(gate_ref[r, lanes]) * o).astype(o_ref.dtype)


def _pool_tile(u_ref, w_ref, sc_ref, o_ref, ext_ref, seq_tile):
    ts = u_ref.shape[0]
    gd = u_ref.shape[-1] // len(POOL_WINDOWS)
    ext_ref[POOL_HALO:POOL_HALO + ts, :] = u_ref[...]
    t = seq_tile * ts + lax.broadcasted_iota(jnp.int32, (ts, 1), 0)
    for g, w in enumerate(POOL_WINDOWS):
        cols = slice(g * gd, (g + 1) * gd)
        u = ext_ref[POOL_HALO:POOL_HALO + ts, cols]
        win = u
        for k in range(1, w):
            win = win + ext_ref[POOL_HALO - k:POOL_HALO - k + ts, cols]
        cnt = jnp.minimum(t + 1, w).astype(F32)
        pooled = win / cnt - u
        mixed = _dot(pooled.astype(BF16), w_ref[g])
        o_ref[:, cols] = (mixed * sc_ref[:, cols]).astype(o_ref.dtype)


def _ret_pool_kernel(u_ref, q_ref, k_ref, gate_ref, v_ref, cos_ref, sin_ref, dm_ref, xi_ref, zeta_ref,
                     cd_ref, pw_ref, psc_ref, yr_ref, yp_ref, state_ref, ext_ref):
    i = pl.program_id(1)
    ts = u_ref.shape[0]

    @pl.when(i == 0)
    def _():
        state_ref[...] = jnp.zeros_like(state_ref)
        ext_ref[0:POOL_HALO, :] = jnp.zeros((POOL_HALO, ext_ref.shape[1]), F32)

    @pl.when(i > 0)
    def _():
        ext_ref[0:POOL_HALO, :] = ext_ref[ts:ts + POOL_HALO, :]

    _pool_tile(u_ref, pw_ref, psc_ref, yp_ref, ext_ref, i)
    _retention_tile(q_ref, k_ref, v_ref, gate_ref, cos_ref, sin_ref, dm_ref, xi_ref, zeta_ref,
                    cd_ref, yr_ref, state_ref)


def _retention_and_pool(za, zb, tables, pool_w, pool_scale, layer, *, width, ts):
    b, s, _ = za.shape
    heads = width // HEAD_DIM
    cos, sin, dmat, xi, zeta, cdec = tables
    tok = lambda blk: pl.BlockSpec((None, ts, width), lambda bi, i: (bi, i, blk))
    whole = lambda a: pl.BlockSpec(a.shape, lambda bi, i: (0, 0, 0))
    pos = pl.BlockSpec((ts, HEAD_DIM), lambda bi, i: (i, 0))
    groups, gd = pool_w.shape[1], pool_w.shape[2]
    out = jax.ShapeDtypeStruct((b, s, width), BF16)
    return pl.pallas_call(
        _ret_pool_kernel,
        grid=(b, s // ts),
        in_specs=[tok(0), tok(1), tok(2), tok(3), tok(3), pos, pos,
                  whole(dmat), whole(xi), whole(zeta), whole(cdec),
                  pl.BlockSpec((None, groups, gd, gd), lambda bi, i: (layer, 0, 0, 0)),
                  pl.BlockSpec((None, 1, width), lambda bi, i: (layer, 0, 0))],
        out_specs=[tok(0), tok(0)],
        out_shape=[out, out],
        scratch_shapes=[pltpu.VMEM((heads, HEAD_DIM, HEAD_DIM), F32),
                        pltpu.VMEM((POOL_HALO + ts, width), F32)],
        compiler_params=_params("parallel", "arbitrary"),
        name="retention_pool",
    )(za, za, za, za, zb, cos, sin, dmat, xi, zeta, cdec, pool_w, pool_scale)


def _retention_tables(s, heads):
    half = HEAD_DIM // 2
    inv_freq = ROPE_BASE ** (-jnp.arange(half, dtype=F32) / half)
    ang = jnp.arange(s, dtype=F32)[:, None] * inv_freq[None, :]
    cos = jnp.concatenate([jnp.cos(ang), jnp.cos(ang)], axis=-1)
    sin = jnp.concatenate([-jnp.sin(ang), jnp.sin(ang)], axis=-1)
    log_gamma = jnp.log1p(-jnp.power(2.0, -5.0 - jnp.arange(heads, dtype=F32)))
    pos = jnp.arange(CHUNK, dtype=F32)
    diff = pos[:, None] - pos[None, :]
    dmat = jnp.where(diff[None] >= 0,
                     jnp.exp(jnp.maximum(diff, 0.0)[None] * log_gamma[:, None, None]), 0.0)
    xi = jnp.exp((pos + 1.0)[None, :] * log_gamma[:, None])
    zeta = jnp.exp((CHUNK - 1.0 - pos)[None, :] * log_gamma[:, None])
    cdec = jnp.exp(CHUNK * log_gamma)
    wide = lambda a: jnp.broadcast_to(a[:, :, None], (heads, CHUNK, HEAD_DIM))
    return (cos, sin, dmat, wide(xi), wide(zeta),
            jnp.broadcast_to(cdec[:, None, None], (heads, SUBLANES, HEAD_DIM)))


def _merge_kernel(*refs, convert_next):
    yp_ref, yf_ref, yr_ref, g0_ref, g1_ref, g2_ref, x_ref, wp_ref, wf_ref, wr_ref, wo_ref = refs[:11]
    n = 4 if convert_next else 0
    next_refs, o_ref, cast_refs = refs[11:11 + n], refs[11 + n], refs[12 + n:]
    merged = (jax.nn.sigmoid(g0_ref[...]) * _dot(yp_ref[...], wp_ref[...])
              + jax.nn.sigmoid(g1_ref[...]) * _dot(yf_ref[...], wf_ref[...])
              + jax.nn.sigmoid(g2_ref[...]) * _dot(yr_ref[...], wr_ref[...]))
    o_ref[...] = x_ref[...] + _dot(merged.astype(BF16), wo_ref[...])
    for src_ref, dst_ref in zip(next_refs, cast_refs):
        dst_ref[...] = src_ref[...].astype(BF16)


def _merge(y_pool, y_fox, y_ret, za, x, weights, *, gate_blk, next_weights=None, tm=MERGE_TM):
    t, d = x.shape
    gm = t // tm
    branch = lambda y: pl.BlockSpec((tm, y.shape[-1]), lambda m: (m, 0))
    gate = lambda i: pl.BlockSpec((tm, d), lambda m: (m, gate_blk + i))
    resident = lambda w: pl.BlockSpec(w.shape, lambda m: (0, 0), pipeline_mode=pl.Buffered(1))
    in_specs = [branch(y_pool), branch(y_fox), branch(y_ret), gate(0), gate(1), gate(2),
                pl.BlockSpec((tm, d), lambda m: (m, 0))] + [resident(w) for w in weights]
    args = [y_pool, y_fox, y_ret, za, za, za, x, *weights]
    out_specs = [pl.BlockSpec((tm, d), lambda m: (m, 0))]
    out_shape = [jax.ShapeDtypeStruct((t, d), F32)]
    if next_weights is not None:
        *stacked, nl = next_weights
        for w in stacked:
            rows = w.shape[1] // gm
            in_specs.append(pl.BlockSpec((None, rows, w.shape[2]), lambda m: (nl, m, 0)))
            out_specs.append(pl.BlockSpec((rows, w.shape[2]), lambda m: (m, 0)))
            out_shape.append(jax.ShapeDtypeStruct(w.shape[1:], BF16))
        args += stacked
    outs = pl.pallas_call(
        functools.partial(_merge_kernel, convert_next=next_weights is not None),
        grid=(gm,),
        in_specs=in_specs,
        out_specs=out_specs,
        out_shape=out_shape,
        compiler_params=_params("parallel"),
        name="merge",
    )(*args)
    return outs[0], (tuple(outs[1:]) if next_weights is not None else None)


def kernel(x, ffn1_norm, ffn1_w13, ffn1_w2, mix_norm, w_in, forget_bias, pool_w, pool_scale,
           w_branch_pool, w_branch_fox, w_branch_ret, w_out, ffn2_norm, ffn2_w13, ffn2_w2,
           final_norm):
    b, s, d = x.shape
    depth = w_in.shape[0]
    pool_width = pool_scale.shape[-1]
    fox_heads = forget_bias.shape[-1]
    fox_width = w_branch_fox.shape[1]
    ret_width = w_branch_ret.shape[1]
    ret_heads = ret_width // HEAD_DIM
    assert fox_width == fox_heads * HEAD_DIM and pool_width == fox_width == ret_width
    assert d % fox_width == 0

    o = 0
    cuts = {}
    for name, width in (("pool", pool_width), ("fq", fox_width), ("fk", fox_width),
                        ("fv", fox_width), ("fl", fox_heads), ("rq", ret_width),
                        ("rk", ret_width), ("rv", ret_width), ("rg", ret_width),
                        ("gates", N_BRANCHES * d)):
        cuts[name] = slice(o, o + width)
        o += width
    assert o == w_in.shape[-1]
    w_in_t = jnp.swapaxes(w_in, 1, 2)
    tn = pool_width
    rows_of = lambda name: list(range(cuts[name].start, cuts[name].stop, tn))
    rows_a = sum((rows_of(n) for n in ("pool", "rq", "rk", "rg", "gates")), [])
    rows_b = sum((rows_of(n) for n in ("fq", "fk", "fv", "rv")), [])
    gate_blk = (pool_width + 3 * ret_width) // d

    bias = jnp.pad(forget_bias, ((0, 0), (0, LANES - fox_heads)))[:, None, :]
    row = lambda a: a[:, None, :]
    pool_w_b = pool_w.astype(BF16)
    branch_f32 = (w_branch_pool, w_branch_fox, w_branch_ret, w_out)
    branch_w = tuple(w[0].astype(BF16) for w in branch_f32)
    tables = _retention_tables(s, ret_heads)

    halves = [(norm, w13, w2, l) for l in range(depth)
              for norm, w13, w2 in ((ffn1_norm, ffn1_w13, ffn1_w2), (ffn2_norm, ffn2_w13, ffn2_w2))]

    def half_step(xt, i, w13_b, w2_b):
        norm, _, _, l = halves[i]
        last = i == len(halves) - 1
        nxt = None if last else (halves[i + 1][1], halves[i + 1][2], halves[i + 1][3])
        return _ffn(xt, norm[l][None, :], w13_b, w2_b, next_weights=nxt,
                    final_gain=final_norm[None, :] if last else None)

    xt = x.reshape(b * s, d)
    w13_b, w2_b = ffn1_w13[0].astype(BF16), ffn1_w2[0].astype(BF16)
    for l in range(depth):
        xt, w13_b, w2_b = half_step(xt, 2 * l, w13_b, w2_b)
        gain = mix_norm[l][None, :]
        za, zb, fl = _in_proj(xt, gain, w_in_t, l, rows_a, rows_b, cuts["fl"].start, fox_heads, tn=tn)
        za3, zb3 = za.reshape(b, s, -1), zb.reshape(b, s, -1)
        cf, cft = _forget_cumsum(fl.reshape(b, s, LANES), bias, l, heads=fox_heads, tk=FOX_TQ)
        y_fox = _fox(zb3, cf, cft, width=fox_width, tq=FOX_TQ)
        y_ret, y_pool = _retention_and_pool(za3, zb3, tables, pool_w_b, row(pool_scale), l,
                                            width=pool_width, ts=RET_POOL_TS)
        flat = lambda y: y.reshape(b * s, -1)
        xt, branch_w = _merge(flat(y_pool), flat(y_fox), flat(y_ret), za, xt, branch_w,
                              gate_blk=gate_blk,
                              next_weights=(*branch_f32, l + 1) if l + 1 < depth else None)
        xt, w13_b, w2_b = half_step(xt, 2 * l + 1, w13_b, w2_b)
    return xt.reshape(b, s, d)
```

```python
import functools

import jax
import jax.numpy as jnp
from jax import lax
from jax.experimental import pallas as pl
from jax.experimental.pallas import tpu as pltpu

F32 = jnp.float32
BF16 = jnp.bfloat16

RMS_EPS = 1e-6
GN_EPS = 1e-5
ROPE_BASE = 10000.0
POOL_WINDOWS = (2, 4, 8, 16)
POOL_HALO = 16
HEAD_DIM = 128
CHUNK = 128
N_BRANCHES = 3

LANES = 128
SUBLANES = 8
VMEM_LIMIT_BYTES = 60 * 1024 * 1024

FFN_TM, FFN_TF = 1024, 512
IN_PROJ_TM = 1024
MERGE_TM = 256
FOX_TQ = 512
RET_POOL_TS = 4 * CHUNK


def _params(*semantics):
    return pltpu.CompilerParams(dimension_semantics=semantics,
                                vmem_limit_bytes=VMEM_LIMIT_BYTES)


def _rms(x, g):
    return x * lax.rsqrt(jnp.mean(x * x, axis=-1, keepdims=True) + RMS_EPS) * g


def _silu(x):
    return x * jax.nn.sigmoid(x)


def _dot(a, b):
    return jnp.dot(a, b, preferred_element_type=F32)


def _dot_nt(a, b):
    return lax.dot_general(a, b, (((1,), (1,)), ((), ())), preferred_element_type=F32)


def _ffn_kernel(*refs, final_norm, convert_next):
    refs = list(refs)
    x_ref, g_ref, w1_ref, w3_ref, w2_ref = refs[:5]
    del refs[:5]
    gf_ref = refs.pop(0) if final_norm else None
    next_w13_ref, next_w2_ref = (refs.pop(0), refs.pop(0)) if convert_next else (None, None)
    o_ref = refs.pop(0)
    cast_w13_ref, cast_w2_ref = (refs.pop(0), refs.pop(0)) if convert_next else (None, None)
    (xn_ref,) = refs
    f = pl.program_id(1)

    @pl.when(f == 0)
    def _():
        x = x_ref[...]
        xn_ref[...] = _rms(x, g_ref[...]).astype(BF16)
        o_ref[...] = x

    xn = xn_ref[...]
    h = (_silu(_dot(xn, w1_ref[...])) * _dot(xn, w3_ref[...])).astype(BF16)
    o_ref[...] += 0.5 * _dot(h, w2_ref[...])

    if final_norm:
        @pl.when(f == pl.num_programs(1) - 1)
        def _():
            o_ref[...] = _rms(o_ref[...], gf_ref[...])

    if convert_next:
        cast_w13_ref[...] = next_w13_ref[...].astype(BF16)
        cast_w2_ref[...] = next_w2_ref[...].astype(BF16)


def _ffn(x, gain, w13, w2, *, final_gain=None, next_weights=None, tm=FFN_TM, tf=FFN_TF):
    t, d = x.shape
    d_ff = w2.shape[0]
    gm, nf = t // tm, d_ff // tf
    in_specs = [
        pl.BlockSpec((tm, d), lambda m, f: (m, 0)),
        pl.BlockSpec((1, d), lambda m, f: (0, 0)),
        pl.BlockSpec((d, tf), lambda m, f: (0, f)),
        pl.BlockSpec((d, tf), lambda m, f: (0, f + nf)),
        pl.BlockSpec((tf, d), lambda m, f: (f, 0)),
    ]
    args = [x, gain, w13, w13, w2]
    out_specs = [pl.BlockSpec((tm, d), lambda m, f: (m, 0))]
    out_shape = [jax.ShapeDtypeStruct((t, d), F32)]
    if final_gain is not None:
        in_specs.append(pl.BlockSpec((1, d), lambda m, f: (0, 0)))
        args.append(final_gain)
    if next_weights is not None:
        n13, n2, nl = next_weights
        b13 = (d // gm, 2 * d_ff // nf)
        b2 = (d_ff // nf, d // gm)
        in_specs += [pl.BlockSpec((None,) + b13, lambda m, f: (nl, m, f)),
                     pl.BlockSpec((None,) + b2, lambda m, f: (nl, f, m))]
        args += [n13, n2]
        out_specs += [pl.BlockSpec(b13, lambda m, f: (m, f)), pl.BlockSpec(b2, lambda m, f: (f, m))]
        out_shape += [jax.ShapeDtypeStruct(n13.shape[1:], BF16),
                      jax.ShapeDtypeStruct(n2.shape[1:], BF16)]
    outs = pl.pallas_call(
        functools.partial(_ffn_kernel, final_norm=final_gain is not None,
                          convert_next=next_weights is not None),
        grid=(gm, nf),
        in_specs=in_specs,
        out_specs=out_specs,
        out_shape=out_shape,
        scratch_shapes=[pltpu.VMEM((tm, d), BF16)],
        compiler_params=_params("parallel", "arbitrary"),
        name="ffn",
    )(*args)
    return outs if next_weights is not None else (outs[0], None, None)


def _in_proj_kernel(rows_ref, x_ref, g_ref, w_ref, wg_ref, za_ref, zb_ref, fl_ref, xn_ref, *,
                    n_f32, gate_rows):
    del rows_ref
    j = pl.program_id(1)

    @pl.when(j == 0)
    def _():
        xn_ref[...] = _rms(x_ref[...], g_ref[...]).astype(BF16)
        row = lax.broadcasted_iota(jnp.int32, wg_ref.shape[1:], 0)
        wg = jnp.where(row < gate_rows, wg_ref[0], 0.0).astype(BF16)
        fl_ref[...] = _dot_nt(xn_ref[...], wg)

    @pl.when(j < n_f32)
    def _():
        za_ref[...] = _dot_nt(xn_ref[...], w_ref[0].astype(BF16))

    @pl.when(j >= n_f32)
    def _():
        zb_ref[...] = _dot_nt(xn_ref[...], w_ref[0].astype(BF16)).astype(zb_ref.dtype)


def _in_proj(x, gain, w_t, layer, rows_f32, rows_bf16, gate_start, gate_rows, *, tn,
             tm=IN_PROJ_TM):
    t, d = x.shape
    n_f32, n_bf16 = len(rows_f32), len(rows_bf16)
    gm, gn = t // tm, n_f32 + n_bf16
    assert all(r % SUBLANES == 0 for r in list(rows_f32) + list(rows_bf16) + [gate_start])
    table = jnp.asarray(list(rows_f32) + list(rows_bf16), jnp.int32) // SUBLANES
    return pl.pallas_call(
        functools.partial(_in_proj_kernel, n_f32=n_f32, gate_rows=gate_rows),
        grid_spec=pltpu.PrefetchScalarGridSpec(
            num_scalar_prefetch=1, grid=(gm, gn),
            in_specs=[
                pl.BlockSpec((tm, d), lambda m, j, tab: (m, 0)),
                pl.BlockSpec((1, d), lambda m, j, tab: (0, 0)),
                pl.BlockSpec((pl.Element(1), pl.Element(tn), pl.Element(d)),
                             lambda m, j, tab: (layer, tab[j] * SUBLANES, 0)),
                pl.BlockSpec((pl.Element(1), pl.Element(LANES), pl.Element(d)),
                             lambda m, j, tab: (layer, gate_start, 0)),
            ],
            out_specs=[
                pl.BlockSpec((tm, tn), lambda m, j, tab: (m, jnp.minimum(j, n_f32 - 1))),
                pl.BlockSpec((tm, tn), lambda m, j, tab: (m, jnp.maximum(j - n_f32, 0))),
                pl.BlockSpec((tm, LANES), lambda m, j, tab: (m, 0)),
            ],
            scratch_shapes=[pltpu.VMEM((tm, d), BF16)]),
        out_shape=[jax.ShapeDtypeStruct((t, n_f32 * tn), F32),
                   jax.ShapeDtypeStruct((t, n_bf16 * tn), BF16),
                   jax.ShapeDtypeStruct((t, LANES), F32)],
        compiler_params=_params("parallel", "arbitrary"),
        name="in_proj",
    )(table, x, gain, w_t, w_t)


def _forget_cumsum_kernel(fl_ref, b_ref, cf_ref, cft_ref, *, heads, tk):
    z = fl_ref[...] + b_ref[...]
    x = jnp.minimum(z, 0.0) - jnp.log1p(jnp.exp(-jnp.abs(z)))
    s = x.shape[0]
    row = lax.broadcasted_iota(jnp.int32, x.shape, 0)
    shift = 1
    while shift < s:
        if shift < SUBLANES:
            prev = jnp.where(row >= shift, pltpu.roll(x, shift, axis=0), 0.0)
        else:
            prev = jnp.concatenate([jnp.zeros((shift, x.shape[1]), F32), x[:s - shift]], axis=0)
        x = x + prev
        shift *= 2
    cf_ref[...] = x
    xt = x.T
    for j in range(s // tk):
        cft_ref[j] = xt[:heads, j * tk:(j + 1) * tk]


def _forget_cumsum(fl, bias, layer, *, heads, tk):
    b, s, w = fl.shape
    return pl.pallas_call(
        functools.partial(_forget_cumsum_kernel, heads=heads, tk=tk),
        grid=(b,),
        in_specs=[
            pl.BlockSpec((None, s, w), lambda i: (i, 0, 0)),
            pl.BlockSpec((None, 1, w), lambda i: (layer, 0, 0)),
        ],
        out_specs=[
            pl.BlockSpec((None, s, w), lambda i: (i, 0, 0)),
            pl.BlockSpec((None, s // tk, heads, tk), lambda i: (i, 0, 0, 0)),
        ],
        out_shape=[
            jax.ShapeDtypeStruct((b, s, w), F32),
            jax.ShapeDtypeStruct((b, s // tk, heads, tk), F32),
        ],
        compiler_params=_params("parallel"),
        name="forget_cumsum",
    )(fl, bias)


def _fox_kernel(q_ref, k_ref, v_ref, cf_ref, cft_ref, o_ref, fq_ref, m_ref, accl_ref, *, tq):
    i = pl.program_id(1)
    heads = q_ref.shape[-1] // HEAD_DIM
    log2e = 1.4426950408889634
    c1 = HEAD_DIM ** -0.5 * log2e
    ones = jnp.ones((tq, HEAD_DIM), BF16)
    wide = lambda a, lanes: jnp.concatenate([a] * (lanes // LANES), axis=1)

    cf = cf_ref[...] * log2e
    for h in range(heads):
        fq_ref[h] = jnp.broadcast_to(cf[:, h:h + 1], (tq, LANES))
    m_ref[...] = jnp.full(m_ref.shape, -jnp.inf, F32)
    accl_ref[...] = jnp.zeros(accl_ref.shape, F32)

    def block(key0, nk, row0, causal_mask):
        rows = slice(row0, tq)
        for h in range(heads):
            lanes = slice(h * HEAD_DIM, (h + 1) * HEAD_DIM)
            kb = k_ref[pl.ds(key0, nk), lanes]
            v1 = jnp.concatenate([v_ref[pl.ds(key0, nk), lanes], ones[:nk]], axis=1)
            fk = cft_ref[key0 // tq, h:h + 1, pl.ds(key0 % tq, nk)] * log2e
            t = _dot_nt(q_ref[rows, lanes], kb) * c1 + wide(fq_ref[h, rows], nk) - fk
            if causal_mask is not None:
                t = jnp.where(causal_mask, t, -jnp.inf)
            m_prev = m_ref[h, rows]
            m_new = jnp.maximum(m_prev, jnp.max(t, axis=-1, keepdims=True))
            p = jnp.exp2(t - wide(m_new, nk))
            alpha = wide(jnp.exp2(m_prev - m_new), 2 * HEAD_DIM)
            accl_ref[h, rows] = alpha * accl_ref[h, rows] + _dot(p.astype(BF16), v1)
            m_ref[h, rows] = m_new

    full = lambda j: block(pl.multiple_of(j * tq, tq), tq, 0, None)

    def pair(jj, c):
        full(2 * jj)
        full(2 * jj + 1)
        return c

    lax.fori_loop(0, i // 2, pair, 0)

    @pl.when(i % 2 == 1)
    def _():
        full(i - 1)

    half = tq // 2
    d0 = pl.multiple_of(i * tq, tq)
    tri = (lax.broadcasted_iota(jnp.int32, (half, half), 0)
           >= lax.broadcasted_iota(jnp.int32, (half, half), 1))
    block(d0, half, 0, jnp.concatenate([tri, jnp.ones((half, half), jnp.bool_)], axis=0))
    block(d0 + half, half, half, tri)
    for h in range(heads):
        accl = accl_ref[h]
        o_ref[:, h * HEAD_DIM:(h + 1) * HEAD_DIM] = (
            accl[:, :HEAD_DIM] / accl[:, HEAD_DIM:]).astype(o_ref.dtype)


def _fox(zb, cf, cft, *, width, tq):
    b, s, _ = zb.shape
    heads = width // HEAD_DIM
    return pl.pallas_call(
        functools.partial(_fox_kernel, tq=tq),
        grid=(b, s // tq),
        in_specs=[
            pl.BlockSpec((None, tq, width), lambda bi, i: (bi, i, 0)),
            pl.BlockSpec((None, s, width), lambda bi, i: (bi, 0, 1)),
            pl.BlockSpec((None, s, width), lambda bi, i: (bi, 0, 2)),
            pl.BlockSpec((None, tq, LANES), lambda bi, i: (bi, i, 0)),
            pl.BlockSpec((None, s // tq, heads, tq), lambda bi, i: (bi, 0, 0, 0)),
        ],
        out_specs=pl.BlockSpec((None, tq, width), lambda bi, i: (bi, i, 0)),
        out_shape=jax.ShapeDtypeStruct((b, s, width), BF16),
        scratch_shapes=[pltpu.VMEM((heads, tq, LANES), F32),
                        pltpu.VMEM((heads, tq, LANES), F32),
                        pltpu.VMEM((heads, tq, 2 * HEAD_DIM), F32)],
        compiler_params=_params("parallel", "arbitrary"),
        name="fox",
    )(zb, zb, zb, cf, cft)


def _retention_tile(q_ref, k_ref, v_ref, gate_ref, cos_ref, sin_ref, dm_ref, xi_ref, zeta_ref,
                    cd_ref, o_ref, state_ref):
    heads = state_ref.shape[0]
    for c in range(q_ref.shape[0] // CHUNK):
        r = slice(c * CHUNK, (c + 1) * CHUNK)
        cos, sin = cos_ref[r, :], sin_ref[r, :]
        for h in range(heads):
            lanes = slice(h * HEAD_DIM, (h + 1) * HEAD_DIM)
            q, k = q_ref[r, lanes], k_ref[r, lanes]
            q = q * cos + pltpu.roll(q, HEAD_DIM // 2, axis=1) * sin
            k = (k * cos + pltpu.roll(k, HEAD_DIM // 2, axis=1) * sin) * (HEAD_DIM ** -0.5)
            v = v_ref[r, lanes]
            state = state_ref[h]
            scores = _dot_nt(q.astype(BF16), k.astype(BF16)) * dm_ref[h]
            o = _dot(scores.astype(BF16), v)
            o = o + _dot((q * xi_ref[h]).astype(BF16), state.astype(BF16))
            kz_t = (k * zeta_ref[h]).T.astype(BF16)
            state_ref[h] = cd_ref[h, 0:1, :] * state + _dot(kz_t, v)
            mu = jnp.mean(o, axis=-1, keepdims=True)
            var = jnp.mean(jnp.square(o - mu), axis=-1, keepdims=True)
            o = (o - mu) * lax.rsqrt(var + GN_EPS)
            o_ref[r, lanes] = (_silu(gate_ref[r, lanes]) * o).astype(o_ref.dtype)


def _pool_tile(u_ref, w_ref, sc_ref, o_ref, ext_ref, seq_tile):
    ts = u_ref.shape[0]
    gd = u_ref.shape[-1] // len(POOL_WINDOWS)
    ext_ref[POOL_HALO:POOL_HALO + ts, :] = u_ref[...]
    t = seq_tile * ts + lax.broadcasted_iota(jnp.int32, (ts, 1), 0)
    for g, w in enumerate(POOL_WINDOWS):
        cols = slice(g * gd, (g + 1) * gd)
        u = ext_ref[POOL_HALO:POOL_HALO + ts, cols]
        win = u
        for k in range(1, w):
            win = win + ext_ref[POOL_HALO - k:POOL_HALO - k + ts, cols]
        cnt = jnp.minimum(t + 1, w).astype(F32)
        pooled = win / cnt - u
        mixed = _dot(pooled.astype(BF16), w_ref[g])
        o_ref[:, cols] = (mixed * sc_ref[:, cols]).astype(o_ref.dtype)


def _ret_pool_kernel(u_ref, q_ref, k_ref, gate_ref, v_ref, cos_ref, sin_ref, dm_ref, xi_ref, zeta_ref,
                     cd_ref, pw_ref, psc_ref, yr_ref, yp_ref, state_ref, ext_ref):
    i = pl.program_id(1)
    ts = u_ref.shape[0]

    @pl.when(i == 0)
    def _():
        state_ref[...] = jnp.zeros_like(state_ref)
        ext_ref[0:POOL_HALO, :] = jnp.zeros((POOL_HALO, ext_ref.shape[1]), F32)

    @pl.when(i > 0)
    def _():
        ext_ref[0:POOL_HALO, :] = ext_ref[ts:ts + POOL_HALO, :]

    _pool_tile(u_ref, pw_ref, psc_ref, yp_ref, ext_ref, i)
    _retention_tile(q_ref, k_ref, v_ref, gate_ref, cos_ref, sin_ref, dm_ref, xi_ref, zeta_ref,
                    cd_ref, yr_ref, state_ref)


def _retention_and_pool(za, zb, tables, pool_w, pool_scale, layer, *, width, ts):
    b, s, _ = za.shape
    heads = width // HEAD_DIM
    cos, sin, dmat, xi, zeta, cdec = tables
    tok = lambda blk: pl.BlockSpec((None, ts, width), lambda bi, i: (bi, i, blk))
    whole = lambda a: pl.BlockSpec(a.shape, lambda bi, i: (0, 0, 0))
    pos = pl.BlockSpec((ts, HEAD_DIM), lambda bi, i: (i, 0))
    groups, gd = pool_w.shape[1], pool_w.shape[2]
    out = jax.ShapeDtypeStruct((b, s, width), BF16)
    return pl.pallas_call(
        _ret_pool_kernel,
        grid=(b, s // ts),
        in_specs=[tok(0), tok(1), tok(2), tok(3), tok(3), pos, pos,
                  whole(dmat), whole(xi), whole(zeta), whole(cdec),
                  pl.BlockSpec((None, groups, gd, gd), lambda bi, i: (layer, 0, 0, 0)),
                  pl.BlockSpec((None, 1, width), lambda bi, i: (layer, 0, 0))],
        out_specs=[tok(0), tok(0)],
        out_shape=[out, out],
        scratch_shapes=[pltpu.VMEM((heads, HEAD_DIM, HEAD_DIM), F32),
                        pltpu.VMEM((POOL_HALO + ts, width), F32)],
        compiler_params=_params("parallel", "arbitrary"),
        name="retention_pool",
    )(za, za, za, za, zb, cos, sin, dmat, xi, zeta, cdec, pool_w, pool_scale)


def _retention_tables(s, heads):
    half = HEAD_DIM // 2
    inv_freq = ROPE_BASE ** (-jnp.arange(half, dtype=F32) / half)
    ang = jnp.arange(s, dtype=F32)[:, None] * inv_freq[None, :]
    cos = jnp.concatenate([jnp.cos(ang), jnp.cos(ang)], axis=-1)
    sin = jnp.concatenate([-jnp.sin(ang), jnp.sin(ang)], axis=-1)
    log_gamma = jnp.log1p(-jnp.power(2.0, -5.0 - jnp.arange(heads, dtype=F32)))
    pos = jnp.arange(CHUNK, dtype=F32)
    diff = pos[:, None] - pos[None, :]
    dmat = jnp.where(diff[None] >= 0,
                     jnp.exp(jnp.maximum(diff, 0.0)[None] * log_gamma[:, None, None]), 0.0)
    xi = jnp.exp((pos + 1.0)[None, :] * log_gamma[:, None])
    zeta = jnp.exp((CHUNK - 1.0 - pos)[None, :] * log_gamma[:, None])
    cdec = jnp.exp(CHUNK * log_gamma)
    wide = lambda a: jnp.broadcast_to(a[:, :, None], (heads, CHUNK, HEAD_DIM))
    return (cos, sin, dmat, wide(xi), wide(zeta),
            jnp.broadcast_to(cdec[:, None, None], (heads, SUBLANES, HEAD_DIM)))


def _merge_kernel(*refs, convert_next):
    yp_ref, yf_ref, yr_ref, g0_ref, g1_ref, g2_ref, x_ref, wp_ref, wf_ref, wr_ref, wo_ref = refs[:11]
    n = 4 if convert_next else 0
    next_refs, o_ref, cast_refs = refs[11:11 + n], refs[11 + n], refs[12 + n:]
    merged = (jax.nn.sigmoid(g0_ref[...]) * _dot(yp_ref[...], wp_ref[...])
              + jax.nn.sigmoid(g1_ref[...]) * _dot(yf_ref[...], wf_ref[...])
              + jax.nn.sigmoid(g2_ref[...]) * _dot(yr_ref[...], wr_ref[...]))
    o_ref[...] = x_ref[...] + _dot(merged.astype(BF16), wo_ref[...])
    for src_ref, dst_ref in zip(next_refs, cast_refs):
        dst_ref[...] = src_ref[...].astype(BF16)


def _merge(y_pool, y_fox, y_ret, za, x, weights, *, gate_blk, next_weights=None, tm=MERGE_TM):
    t, d = x.shape
    gm = t // tm
    branch = lambda y: pl.BlockSpec((tm, y.shape[-1]), lambda m: (m, 0))
    gate = lambda i: pl.BlockSpec((tm, d), lambda m: (m, gate_blk + i))
    resident = lambda w: pl.BlockSpec(w.shape, lambda m: (0, 0), pipeline_mode=pl.Buffered(1))
    in_specs = [branch(y_pool), branch(y_fox), branch(y_ret), gate(0), gate(1), gate(2),
                pl.BlockSpec((tm, d), lambda m: (m, 0))] + [resident(w) for w in weights]
    args = [y_pool, y_fox, y_ret, za, za, za, x, *weights]
    out_specs = [pl.BlockSpec((tm, d), lambda m: (m, 0))]
    out_shape = [jax.ShapeDtypeStruct((t, d), F32)]
    if next_weights is not None:
        *stacked, nl = next_weights
        for w in stacked:
            rows = w.shape[1] // gm
            in_specs.append(pl.BlockSpec((None, rows, w.shape[2]), lambda m: (nl, m, 0)))
            out_specs.append(pl.BlockSpec((rows, w.shape[2]), lambda m: (m, 0)))
            out_shape.append(jax.ShapeDtypeStruct(w.shape[1:], BF16))
        args += stacked
    outs = pl.pallas_call(
        functools.partial(_merge_kernel, convert_next=next_weights is not None),
        grid=(gm,),
        in_specs=in_specs,
        out_specs=out_specs,
        out_shape=out_shape,
        compiler_params=_params("parallel"),
        name="merge",
    )(*args)
    return outs[0], (tuple(outs[1:]) if next_weights is not None else None)


def kernel(x, ffn1_norm, ffn1_w13, ffn1_w2, mix_norm, w_in, forget_bias, pool_w, pool_scale,
           w_branch_pool, w_branch_fox, w_branch_ret, w_out, ffn2_norm, ffn2_w13, ffn2_w2,
           final_norm):
    b, s, d = x.shape
    depth = w_in.shape[0]
    pool_width = pool_scale.shape[-1]
    fox_heads = forget_bias.shape[-1]
    fox_width = w_branch_fox.shape[1]
    ret_width = w_branch_ret.shape[1]
    ret_heads = ret_width // HEAD_DIM
    assert fox_width == fox_heads * HEAD_DIM and pool_width == fox_width == ret_width
    assert d % fox_width == 0

    o = 0
    cuts = {}
    for name, width in (("pool", pool_width), ("fq", fox_width), ("fk", fox_width),
                        ("fv", fox_width), ("fl", fox_heads), ("rq", ret_width),
                        ("rk", ret_width), ("rv", ret_width), ("rg", ret_width),
                        ("gates", N_BRANCHES * d)):
        cuts[name] = slice(o, o + width)
        o += width
    assert o == w_in.shape[-1]
    w_in_t = jnp.swapaxes(w_in, 1, 2)
    tn = pool_width
    rows_of = lambda name: list(range(cuts[name].start, cuts[name].stop, tn))
    rows_a = sum((rows_of(n) for n in ("pool", "rq", "rk", "rg", "gates")), [])
    rows_b = sum((rows_of(n) for n in ("fq", "fk", "fv", "rv")), [])
    gate_blk = (pool_width + 3 * ret_width) // d

    bias = jnp.pad(forget_bias, ((0, 0), (0, LANES - fox_heads)))[:, None, :]
    row = lambda a: a[:, None, :]
    pool_w_b = pool_w.astype(BF16)
    branch_f32 = (w_branch_pool, w_branch_fox, w_branch_ret, w_out)
    branch_w = tuple(w[0].astype(BF16) for w in branch_f32)
    tables = _retention_tables(s, ret_heads)

    halves = [(norm, w13, w2, l) for l in range(depth)
              for norm, w13, w2 in ((ffn1_norm, ffn1_w13, ffn1_w2), (ffn2_norm, ffn2_w13, ffn2_w2))]

    def half_step(xt, i, w13_b, w2_b):
        norm, _, _, l = halves[i]
        last = i == len(halves) - 1
        nxt = None if last else (halves[i + 1][1], halves[i + 1][2], halves[i + 1][3])
        return _ffn(xt, norm[l][None, :], w13_b, w2_b, next_weights=nxt,
                    final_gain=final_norm[None, :] if last else None)

    xt = x.reshape(b * s, d)
    w13_b, w2_b = ffn1_w13[0].astype(BF16), ffn1_w2[0].astype(BF16)
    for l in range(depth):
        xt, w13_b, w2_b = half_step(xt, 2 * l, w13_b, w2_b)
        gain = mix_norm[l][None, :]
        za, zb, fl = _in_proj(xt, gain, w_in_t, l, rows_a, rows_b, cuts["fl"].start, fox_heads, tn=tn)
        za3, zb3 = za.reshape(b, s, -1), zb.reshape(b, s, -1)
        cf, cft = _forget_cumsum(fl.reshape(b, s, LANES), bias, l, heads=fox_heads, tk=FOX_TQ)
        y_fox = _fox(zb3, cf, cft, width=fox_width, tq=FOX_TQ)
        y_ret, y_pool = _retention_and_pool(za3, zb3, tables, pool_w_b, row(pool_scale), l,
                                            width=pool_width, ts=RET_POOL_TS)
        flat = lambda y: y.reshape(b * s, -1)
        xt, branch_w = _merge(flat(y_pool), flat(y_fox), flat(y_ret), za, xt, branch_w,
                              gate_blk=gate_blk,
                              next_weights=(*branch_f32, l + 1) if l + 1 < depth else None)
        xt, w13_b, w2_b = half_step(xt, 2 * l + 1, w13_b, w2_b)
    return xt.reshape(b, s, d)
```
